```python
import math
import jax, jax.numpy as jnp
from jax import lax
import numpy as np

D_MODEL = 1024
BATCH = 8
SEQ = 8192
DEPTH = 1
DEC_BATCH = 8
DEC_SEQ = 32
PAST_LEN = 1024

CHUNK = 64
Q_BLOCK = 128
N_HEADS = 8
HEAD_DIM = 64
V_DIM = 2 * HEAD_DIM
ATTN_DIM = N_HEADS * V_DIM
CONV_DIM = D_MODEL
CONV_WIDTH = 3
D_FF = -(-8 * D_MODEL // (3 * 256)) * 256
EPS = 1e-6
IN_SIZES = (N_HEADS * 2 * HEAD_DIM,
            N_HEADS * 2 * HEAD_DIM,
            ATTN_DIM,
            CONV_DIM,
            CONV_DIM,
            CONV_DIM,
            D_MODEL,
            D_MODEL)
IN_COLS = sum(IN_SIZES)
IN_SPLITS = tuple(int(s) for s in np.cumsum(IN_SIZES)[:-1])

kernel_name = "hybrid_diffattn_shortconv_streaming_step"


def _rms_norm(x, g):
    xf = x.astype(jnp.float32)
    y = xf * lax.rsqrt(jnp.mean(xf * xf, axis=-1, keepdims=True) + EPS)
    return (y * g.astype(jnp.float32)).astype(x.dtype)


def _lambda_init(layer_idx):
    return 0.8 - 0.6 * math.exp(-0.3 * layer_idx)


def _diff_core(q, k, v, mask, lam):
    s = jnp.einsum('bqhmd,bkhmd->bhmqk', q, k,
                   preferred_element_type=jnp.float32) * (HEAD_DIM ** -0.5)
    if mask is not None:
        s = jnp.where(mask, s, -jnp.inf)
    p = jax.nn.softmax(s, axis=-1)
    a = p[:, :, 0] - lam * p[:, :, 1]
    return jnp.einsum('bhqk,bkhd->bqhd', a.astype(v.dtype), v)


def _attend_prompt(q, k, v, lam):
    B, S = q.shape[0], q.shape[1]
    nb = S // Q_BLOCK
    qb = q.reshape(B, nb, Q_BLOCK, N_HEADS, 2, HEAD_DIM).swapaxes(0, 1)
    kpos = jnp.arange(S)

    def one(args):
        i, qi = args
        qpos = i * Q_BLOCK + jnp.arange(Q_BLOCK)
        limit = (qpos // CHUNK + 1) * CHUNK
        mask = kpos[None, :] < limit[:, None]
        return _diff_core(qi, k, v, mask, lam)

    out = lax.map(one, (jnp.arange(nb), qb))
    return out.swapaxes(0, 1).reshape(B, S, N_HEADS, V_DIM)


def _short_conv(u, prev, w):
    T = u.shape[1]
    up = jnp.concatenate([prev, u], axis=1)
    y = up[:, 0:T] * w[0]
    for j in range(1, CONV_WIDTH):
        y = y + up[:, j:j + T] * w[j]
    return y, up[:, -(CONV_WIDTH - 1):]


def _layer(x, c, past_k, past_v, conv_prev, lambda_init,
           w_ada, b_ada, norm1_g, norm2_g, w_in, q_norm_g, k_norm_g,
           lambda_q1, lambda_k1, lambda_q2, lambda_k2, sub_norm_g,
           w_attn_out, conv_w, w_conv_out, w_out, w_gate_up, w_down):
    B, T = x.shape[0], x.shape[1]
    mod = (jax.nn.silu(c) @ w_ada + b_ada).reshape(B, 6, 1, D_MODEL)
    shift1, scale1, gate1, shift2, scale2, gate2 = [mod[:, i] for i in range(6)]

    h = _rms_norm(x, norm1_g) * (1 + scale1) + shift1
    proj = h @ w_in
    q, k, v, xin, gb, gc, ga_br, gb_br = jnp.split(proj, IN_SPLITS, axis=-1)

    q = _rms_norm(q.reshape(B, T, N_HEADS, 2, HEAD_DIM), q_norm_g)
    k = _rms_norm(k.reshape(B, T, N_HEADS, 2, HEAD_DIM), k_norm_g)
    v = v.reshape(B, T, N_HEADS, V_DIM)
    lam = (jnp.exp(jnp.sum(lambda_q1.astype(jnp.float32) * lambda_k1.astype(jnp.float32)))
           - jnp.exp(jnp.sum(lambda_q2.astype(jnp.float32) * lambda_k2.astype(jnp.float32)))
           + lambda_init)
    if past_k is None:
        o = _attend_prompt(q, k, v, lam)
    else:
        k_all = jnp.concatenate([past_k, k], axis=1)
        v_all = jnp.concatenate([past_v, v], axis=1)
        o = _diff_core(q, k_all, v_all, None, lam)
    o = _rms_norm(o, sub_norm_g) * (1.0 - lambda_init)
    y_a = o.reshape(B, T, ATTN_DIM) @ w_attn_out

    u = gc * xin
    cv, conv_tail = _short_conv(u, conv_prev, conv_w)
    y_b = (gb * cv) @ w_conv_out

    m = jax.nn.sigmoid(ga_br) * y_a + jax.nn.sigmoid(gb_br) * y_b
    x = x + gate1 * (m @ w_out)

    h2 = _rms_norm(x, norm2_g) * (1 + scale2) + shift2
    g, up = jnp.split(h2 @ w_gate_up, 2, axis=-1)
    x = x + gate2 * ((jax.nn.silu(g) * up) @ w_down)
    return x, k, v, conv_tail


def setup_inputs(seed: int = 0) -> dict:
    key = jax.random.key(seed)
    ks = jax.random.split(key, 28)
    f32 = jnp.float32

    def nrm(k, shape, scale):
        return jax.random.normal(k, shape, f32) * scale

    return {
        "x_prompt": nrm(ks[0], (BATCH, SEQ, D_MODEL), 1.0),
        "x_sample": nrm(ks[1], (DEC_BATCH, DEC_SEQ, D_MODEL), 1.0),
        "cache_k": nrm(ks[2], (DEPTH, DEC_BATCH, PAST_LEN, N_HEADS, 2, HEAD_DIM), 1.0),
        "cache_v": nrm(ks[3], (DEPTH, DEC_BATCH, PAST_LEN, N_HEADS, V_DIM), 1.0),
        "state_conv": nrm(ks[4], (DEPTH, DEC_BATCH, CONV_WIDTH - 1, CONV_DIM), 1.0),
        "c_prompt": nrm(ks[5], (BATCH, D_MODEL), 1.0),
        "c_sample": nrm(ks[6], (DEC_BATCH, D_MODEL), 1.0),
        "w_ada": nrm(ks[7], (DEPTH, D_MODEL, 6 * D_MODEL), 0.5 * D_MODEL ** -0.5),
        "b_ada": nrm(ks[8], (DEPTH, 6 * D_MODEL), 0.02),
        "norm1_g": 1.0 + nrm(ks[9], (DEPTH, D_MODEL), 0.02),
        "norm2_g": 1.0 + nrm(ks[10], (DEPTH, D_MODEL), 0.02),
        "w_in": nrm(ks[11], (DEPTH, D_MODEL, IN_COLS), D_MODEL ** -0.5),
        "q_norm_g": 1.0 + nrm(ks[12], (DEPTH, HEAD_DIM), 0.02),
        "k_norm_g": 1.0 + nrm(ks[13], (DEPTH, HEAD_DIM), 0.02),
        "lambda_q1": nrm(ks[14], (DEPTH, HEAD_DIM), 0.1),
        "lambda_k1": nrm(ks[15], (DEPTH, HEAD_DIM), 0.1),
        "lambda_q2": nrm(ks[16], (DEPTH, HEAD_DIM), 0.1),
        "lambda_k2": nrm(ks[17], (DEPTH, HEAD_DIM), 0.1),
        "sub_norm_g": 1.0 + nrm(ks[18], (DEPTH, V_DIM), 0.02),
        "w_attn_out": nrm(ks[19], (DEPTH, ATTN_DIM, D_MODEL), ATTN_DIM ** -0.5),
        "conv_w": nrm(ks[20], (DEPTH, CONV_WIDTH, CONV_DIM), CONV_WIDTH ** -0.5),
        "w_conv_out": nrm(ks[21], (DEPTH, CONV_DIM, D_MODEL), CONV_DIM ** -0.5),
        "w_out": nrm(ks[22], (DEPTH, D_MODEL, D_MODEL), D_MODEL ** -0.5),
        "w_gate_up": nrm(ks[23], (DEPTH, D_MODEL, 2 * D_FF), D_MODEL ** -0.5),
        "w_down": nrm(ks[24], (DEPTH, D_FF, D_MODEL), D_FF ** -0.5),
    }


def reference(x_prompt, x_sample, cache_k, cache_v, state_conv, c_prompt, c_sample,
              w_ada, b_ada, norm1_g, norm2_g, w_in, q_norm_g, k_norm_g,
              lambda_q1, lambda_k1, lambda_q2, lambda_k2, sub_norm_g,
              w_attn_out, conv_w, w_conv_out, w_out, w_gate_up, w_down):
    xp, xs = x_prompt, x_sample
    kp_l, vp_l, cp_l, ksl, vsl, csl = [], [], [], [], [], []
    for l in range(DEPTH):
        lam0 = _lambda_init(l)
        wl = (w_ada[l], b_ada[l], norm1_g[l], norm2_g[l], w_in[l], q_norm_g[l], k_norm_g[l],
              lambda_q1[l], lambda_k1[l], lambda_q2[l], lambda_k2[l], sub_norm_g[l],
              w_attn_out[l], conv_w[l], w_conv_out[l], w_out[l], w_gate_up[l], w_down[l])
        zeros_prev = jnp.zeros((xp.shape[0], CONV_WIDTH - 1, CONV_DIM), xp.dtype)
        xp, kp, vp, cp = _layer(xp, c_prompt, None, None, zeros_prev, lam0, *wl)
        xs, ksn, vsn, csn = _layer(xs, c_sample, cache_k[l], cache_v[l], state_conv[l], lam0, *wl)
        kp_l.append(kp); vp_l.append(vp); cp_l.append(cp)
        ksl.append(ksn); vsl.append(vsn); csl.append(csn)
    k_prompt = jnp.stack(kp_l)
    v_prompt = jnp.stack(vp_l)
    conv_prompt = jnp.stack(cp_l)
    k_sample = jnp.stack(ksl)
    v_sample = jnp.stack(vsl)
    conv_sample = jnp.stack(csl)
    return (xp, xs, k_prompt, v_prompt, conv_prompt, k_sample, v_sample, conv_sample)
```

```python
import functools
import math

import jax
import jax.numpy as jnp
from jax import lax
from jax.experimental import pallas as pl
from jax.experimental.pallas import tpu as pltpu

D_MODEL = 1024
N_HEADS = 8
HEAD_DIM = 64
V_DIM = 2 * HEAD_DIM
CONV_WIDTH = 3
CHUNK = 64
D_FF = 2816
EPS = 1e-6
N_IN_GROUPS = 8
LOG2E = math.log2(math.e)
NEG_BIG = -1e30

LANES = 128
MXU_DIM = 256
VMEM_LIMIT = 56 * 1024 * 1024

F32 = jnp.float32
BF16 = jnp.bfloat16


def _lambda_init(layer_idx):
    return 0.8 - 0.6 * math.exp(-0.3 * layer_idx)


def _params(*sem):
    return pltpu.CompilerParams(dimension_semantics=sem, vmem_limit_bytes=VMEM_LIMIT)


def _resident(shape):
    return pl.BlockSpec(shape, lambda *_: (0,) * len(shape), pipeline_mode=pl.Buffered(1))


def _dot(a, b):
    return jnp.dot(a, b, preferred_element_type=F32)


def _dot_nt(a, b):
    return lax.dot_general(a, b, (((1,), (1,)), ((), ())), preferred_element_type=F32)


def _rms_rows(x):
    return x * lax.rsqrt(jnp.mean(x * x, axis=-1, keepdims=True) + EPS)


def _mod_kernel(c_ref, w_ref, b_ref, o_ref):
    c = c_ref[...]
    sc = (c * jax.nn.sigmoid(c)).astype(BF16)
    o_ref[...] = _dot(sc, w_ref[...]) + b_ref[...]


def _modulation(c_all, w_ada, b_ada):
    n = c_all.shape[0]
    return pl.pallas_call(
        _mod_kernel,
        grid=(6,),
        in_specs=[pl.BlockSpec((n, D_MODEL), lambda j: (0, 0)),
                  pl.BlockSpec((D_MODEL, D_MODEL), lambda j: (0, j)),
                  pl.BlockSpec((1, D_MODEL), lambda j: (0, j))],
        out_specs=pl.BlockSpec((n, D_MODEL), lambda j: (0, j)),
        out_shape=jax.ShapeDtypeStruct((n, 6 * D_MODEL), F32),
        compiler_params=_params("arbitrary"),
        name="adaln_mod",
    )(c_all, w_ada, b_ada)


def _group_mean_matrix():
    r = lax.broadcasted_iota(jnp.int32, (MXU_DIM, MXU_DIM), 0) // HEAD_DIM
    c = lax.broadcasted_iota(jnp.int32, (MXU_DIM, MXU_DIM), 1) // HEAD_DIM
    return jnp.where(r == c, 1.0 / HEAD_DIM, 0.0).astype(BF16)


def _group_rms(r, gmat):
    r2 = (r * r).astype(BF16)
    ms = jnp.concatenate(
        [_dot(r2[:, c * MXU_DIM:(c + 1) * MXU_DIM], gmat) for c in range(D_MODEL // MXU_DIM)], axis=1)
    return r * lax.rsqrt(ms + EPS)


def _inproj_kernel(x_ref, mod_ref, prev_ref, n1g_ref, w_ref, gq_ref, gk_ref, cw_ref,
                   qb_ref, kf_ref, kb_ref, vf_ref, vb_ref, yb_ref, sa_ref, sb_ref, tail_ref,
                   carry_ref):
    t = pl.program_id(1)
    tm = x_ref.shape[1]

    x = x_ref[0]
    h = (_rms_rows(x) * n1g_ref[...]) * (1.0 + mod_ref[0, 1:2, :]) + mod_ref[0, 0:1, :]
    h = h.astype(BF16)

    def proj(j):
        return _dot(h, w_ref[:, j * D_MODEL:(j + 1) * D_MODEL])

    gmat = _group_mean_matrix()

    qb_ref[0] = (_group_rms(proj(0), gmat) * gq_ref[...]).astype(BF16)

    kn = _group_rms(proj(1), gmat) * gk_ref[...]
    kf_ref[0] = kn
    kb_ref[0] = kn.astype(BF16)

    v = proj(2)
    vf_ref[0] = v
    vb_ref[0] = v.astype(BF16)

    @pl.when(t == 0)
    def _():
        carry_ref[0:2, :] = prev_ref[0]

    u = proj(5) * proj(3)
    p0 = carry_ref[0:1, :]
    p1 = carry_ref[1:2, :]
    row = lax.broadcasted_iota(jnp.int32, (tm, D_MODEL), 0)
    u1 = jnp.where(row == 0, p1, pltpu.roll(u, 1, 0))
    u2 = jnp.where(row == 0, p0, jnp.where(row == 1, p1, pltpu.roll(u, 2, 0)))
    cv = u2 * cw_ref[0:1, :] + u1 * cw_ref[1:2, :] + u * cw_ref[2:3, :]
    tail = u[tm - 2:tm, :]
    carry_ref[0:2, :] = tail
    tail_ref[0] = tail
    yb_ref[0] = (proj(4) * cv).astype(BF16)

    sa_ref[0] = jax.nn.sigmoid(proj(6)).astype(BF16)
    sb_ref[0] = jax.nn.sigmoid(proj(7)).astype(BF16)


def _in_projection(x, mod, conv_prev, n1g, w_in, gq, gk, conv_w, tm):
    B, T, _ = x.shape
    tile = pl.BlockSpec((1, tm, D_MODEL), lambda b, t: (b, t, 0))
    per_b2 = pl.BlockSpec((1, CONV_WIDTH - 1, D_MODEL), lambda b, t: (b, 0, 0))
    row = _resident((1, D_MODEL))
    act_bf = jax.ShapeDtypeStruct((B, T, D_MODEL), BF16)
    act_f32 = jax.ShapeDtypeStruct((B, T, D_MODEL), F32)
    return pl.pallas_call(
        _inproj_kernel,
        grid=(B, T // tm),
        in_specs=[tile,
                  pl.BlockSpec((1, 6, D_MODEL), lambda b, t: (b, 0, 0)),
                  per_b2,
                  row,
                  _resident((D_MODEL, N_IN_GROUPS * D_MODEL)),
                  row, row,
                  _resident((CONV_WIDTH, D_MODEL))],
        out_specs=[tile] * 8 + [per_b2],
        out_shape=[act_bf, act_f32, act_bf, act_f32, act_bf, act_bf, act_bf, act_bf,
                   jax.ShapeDtypeStruct((B, CONV_WIDTH - 1, D_MODEL), F32)],
        scratch_shapes=[pltpu.VMEM((8, D_MODEL), F32)],
        compiler_params=_params("arbitrary", "arbitrary"),
        name="in_projection",
    )(x, mod, conv_prev, n1g, w_in, gq, gk, conv_w)


def _stack_maps(q):
    lane = lax.broadcasted_iota(jnp.int32, q.shape, 1)
    zero = jnp.zeros_like(q)
    return jnp.concatenate([jnp.where(lane < HEAD_DIM, q, zero), jnp.where(lane >= HEAD_DIM, q, zero)], axis=0)


def _lambda_full(lam_ref, lam0):
    lv = lam_ref[...]
    a = jnp.sum(lv[0:1, :] * lv[1:2, :], axis=-1, keepdims=True)
    b = jnp.sum(lv[2:3, :] * lv[3:4, :], axis=-1, keepdims=True)
    return jnp.exp(a) - jnp.exp(b) + lam0


def _finish_head(o1, o2, lam, subg, lam0):
    o = o1 - lam * o2
    return (_rms_rows(o) * subg) * (1.0 - lam0)


def _attn_prompt_kernel(q_ref, k_ref, v_ref, lam_ref, subg_ref, o_ref, m_scr, l_scr, acc_scr, *, tk, lam0):
    qi = pl.program_id(2)
    tq = q_ref.shape[1]
    rows = 2 * tq
    qs = _stack_maps(q_ref[0])

    m_scr[...] = jnp.full(m_scr.shape, NEG_BIG, F32)
    l_scr[...] = jnp.zeros(l_scr.shape, F32)
    acc_scr[...] = jnp.zeros(acc_scr.shape, F32)

    def tile(k0, mask):
        k = k_ref[0, pl.ds(k0, tk), :]
        v = v_ref[0, pl.ds(k0, tk), :]
        s = _dot_nt(qs, k)
        if mask is not None:
            s = jnp.where(mask, s, NEG_BIG)
        m_prev = m_scr[...]
        m_next = jnp.maximum(m_prev, jnp.max(s, axis=1, keepdims=True))
        alpha = jnp.exp2(m_prev - m_next)
        p = jnp.exp2(s - pltpu.repeat(m_next, tk // LANES, axis=1))
        l_scr[...] = alpha * l_scr[...] + jnp.sum(p, axis=1, keepdims=True)
        acc_scr[...] = alpha * acc_scr[...] + _dot(p.astype(BF16), v)
        m_scr[...] = m_next

    n_full = qi * (tq // tk)

    def body(j, carry):
        tile(pl.multiple_of(j * tk, tk), None)
        return carry

    lax.fori_loop(0, n_full, body, 0)

    qchunk = (lax.broadcasted_iota(jnp.int32, (rows, tk), 0) % tq) // CHUNK
    kchunk = lax.broadcasted_iota(jnp.int32, (rows, tk), 1) // CHUNK
    for d in range(tq // tk):
        tile(pl.multiple_of(qi * tq + d * tk, tk), (kchunk + d * (tk // CHUNK)) <= qchunk)

    inv_l = 1.0 / l_scr[...]
    o = acc_scr[...] * inv_l
    lam = _lambda_full(lam_ref, lam0)
    o_ref[0] = _finish_head(o[:tq], o[tq:], lam, subg_ref[...], lam0).astype(o_ref.dtype)


def _attention_prompt(qb, kb, vb, lam_vecs, subg, lam0, tq, tk):
    B, S, _ = qb.shape
    kv_spec = pl.BlockSpec((1, S, V_DIM), lambda b, h, i: (b, 0, h))
    q_spec = pl.BlockSpec((1, tq, V_DIM), lambda b, h, i: (b, i, h))
    return pl.pallas_call(
        functools.partial(_attn_prompt_kernel, tk=tk, lam0=lam0),
        grid=(B, N_HEADS, S // tq),
        in_specs=[q_spec, kv_spec, kv_spec,
                  pl.BlockSpec((4, HEAD_DIM), lambda b, h, i: (0, 0)),
                  pl.BlockSpec((1, V_DIM), lambda b, h, i: (0, 0))],
        out_specs=q_spec,
        out_shape=jax.ShapeDtypeStruct((B, S, D_MODEL), BF16),
        scratch_shapes=[pltpu.VMEM((2 * tq, LANES), F32),
                        pltpu.VMEM((2 * tq, LANES), F32),
                        pltpu.VMEM((2 * tq, V_DIM), F32)],
        compiler_params=_params("arbitrary", "arbitrary", "arbitrary"),
        name="attn_prompt",
    )(qb, kb, vb, lam_vecs, subg)


def _attn_sample_kernel(q_ref, kn_ref, vn_ref, kc_ref, vc_ref, lam_ref, subg_ref, o_ref, *, lam0):
    tq = q_ref.shape[1]
    qs = _stack_maps(q_ref[0])
    s_c = _dot_nt(qs, kc_ref[0].astype(BF16))
    s_n = _dot_nt(qs, kn_ref[0])
    m = jnp.maximum(jnp.max(s_c, axis=1, keepdims=True), jnp.max(s_n, axis=1, keepdims=True))
    p_c = jnp.exp2(s_c - m)
    p_n = jnp.exp2(s_n - m)
    l = jnp.sum(p_c, axis=1, keepdims=True) + jnp.sum(p_n, axis=1, keepdims=True)
    acc = _dot(p_c.astype(BF16), vc_ref[0].astype(BF16)) + _dot(p_n.astype(BF16), vn_ref[0])
    o = acc / l
    lam = _lambda_full(lam_ref, lam0)
    o_ref[0] = _finish_head(o[:tq], o[tq:], lam, subg_ref[...], lam0).astype(o_ref.dtype)


def _attention_sample(qb, kb, vb, cache_k, cache_v, lam_vecs, subg, lam0):
    B, T, _ = qb.shape
    P = cache_k.shape[1]
    new_spec = pl.BlockSpec((1, T, V_DIM), lambda b, h: (b, 0, h))
    past_spec = pl.BlockSpec((1, P, V_DIM), lambda b, h: (b, 0, h))
    return pl.pallas_call(
        functools.partial(_attn_sample_kernel, lam0=lam0),
        grid=(B, N_HEADS),
        in_specs=[new_spec, new_spec, new_spec, past_spec, past_spec,
                  pl.BlockSpec((4, HEAD_DIM), lambda b, h: (0, 0)),
                  pl.BlockSpec((1, V_DIM), lambda b, h: (0, 0))],
        out_specs=new_spec,
        out_shape=jax.ShapeDtypeStruct((B, T, D_MODEL), BF16),
        compiler_params=_params("arbitrary", "arbitrary"),
        name="attn_sample",
    )(qb, kb, vb, cache_k, cache_v, lam_vecs, subg)


def _merge_kernel(x_ref, mod_ref, o_ref, yb_ref, sa_ref, sb_ref, wa_ref, wc_ref, wo_ref, x1_ref):
    y_a = _dot(o_ref[0], wa_ref[...])
    y_b = _dot(yb_ref[0], wc_ref[...])
    m = sa_ref[0].astype(F32) * y_a + sb_ref[0].astype(F32) * y_b
    x1_ref[0] = x_ref[0] + mod_ref[0, 2:3, :] * _dot(m.astype(BF16), wo_ref[...])


def _merge(x, mod, ob, yb, sa, sb, wa, wc, wo, tm):
    B, T, _ = x.shape
    tile = pl.BlockSpec((1, tm, D_MODEL), lambda b, t: (b, t, 0))
    wspec = _resident((D_MODEL, D_MODEL))
    return pl.pallas_call(
        _merge_kernel,
        grid=(B, T // tm),
        in_specs=[tile, pl.BlockSpec((1, 6, D_MODEL), lambda b, t: (b, 0, 0)),
                  tile, tile, tile, tile, wspec, wspec, wspec],
        out_specs=tile,
        out_shape=jax.ShapeDtypeStruct((B, T, D_MODEL), F32),
        compiler_params=_params("arbitrary", "arbitrary"),
        name="branch_merge",
    )(x, mod, ob, yb, sa, sb, wa, wc, wo)


def _ffn_chunks():
    step = 2 * MXU_DIM
    return [(c0, min(step, D_FF - c0)) for c0 in range(0, D_FF, step)]


def _ffn_kernel(x_ref, mod_ref, n2g_ref, wgu_ref, wd_ref, y_ref):
    x1 = x_ref[0]
    h2 = (_rms_rows(x1) * n2g_ref[...]) * (1.0 + mod_ref[0, 4:5, :]) + mod_ref[0, 3:4, :]
    h2 = h2.astype(BF16)
    acc = jnp.zeros(x1.shape, F32)
    for c0, cs in _ffn_chunks():
        g = _dot(h2, wgu_ref[:, c0:c0 + cs])
        up = _dot(h2, wgu_ref[:, D_FF + c0:D_FF + c0 + cs])
        a = ((g * jax.nn.sigmoid(g)) * up).astype(BF16)
        acc = acc + _dot(a, wd_ref[c0:c0 + cs, :])
    y_ref[0] = x1 + mod_ref[0, 5:6, :] * acc


def _ffn(x1, mod, n2g, wgu, wd, tm):
    B, T, _ = x1.shape
    tile = pl.BlockSpec((1, tm, D_MODEL), lambda b, t: (b, t, 0))
    return pl.pallas_call(
        _ffn_kernel,
        grid=(B, T // tm),
        in_specs=[tile, pl.BlockSpec((1, 6, D_MODEL), lambda b, t: (b, 0, 0)),
                  _resident((1, D_MODEL)),
                  _resident((D_MODEL, 2 * D_FF)),
                  _resident((D_FF, D_MODEL))],
        out_specs=tile,
        out_shape=jax.ShapeDtypeStruct((B, T, D_MODEL), F32),
        compiler_params=_params("arbitrary", "arbitrary"),
        name="swiglu_ffn",
    )(x1, mod, n2g, wgu, wd)


def _tile_rows(T):
    return min(T, 512)


def _layer(x, mod, past_k, past_v, conv_prev, lam0, w, attn_tiles):
    B, T, _ = x.shape
    tm = _tile_rows(T)
    qb, kf, kb, vf, vb, yb, sa, sb, tail = _in_projection(
        x, mod, conv_prev, w["n1g"], w["w_in"], w["gq"], w["gk"], w["conv_w"], tm)
    if past_k is None:
        ob = _attention_prompt(qb, kb, vb, w["lam_vecs"], w["subg"], lam0, *attn_tiles)
    else:
        ob = _attention_sample(qb, kb, vb, past_k, past_v, w["lam_vecs"], w["subg"], lam0)
    x1 = _merge(x, mod, ob, yb, sa, sb, w["wa"], w["wc"], w["wo"], tm)
    y = _ffn(x1, mod, w["n2g"], w["wgu"], w["wd"], tm)
    return (y, kf.reshape(B, T, N_HEADS, 2, HEAD_DIM), vf.reshape(B, T, N_HEADS, V_DIM), tail)


def kernel(x_prompt, x_sample, cache_k, cache_v, state_conv, c_prompt, c_sample, w_ada, b_ada, norm1_g, norm2_g, w_in, q_norm_g, k_norm_g, lambda_q1, lambda_k1, lambda_q2, lambda_k2, sub_norm_g, w_attn_out, conv_w, w_conv_out, w_out, w_gate_up, w_down):
    depth = w_ada.shape[0]
    B = x_prompt.shape[0]
    Bs, _, _ = x_sample.shape
    past = cache_k.shape[2]
    xp, xs = x_prompt, x_sample
    outs = [[] for _ in range(6)]
    c_all = jnp.concatenate([c_prompt, c_sample], axis=0)
    for l in range(depth):
        lam0 = _lambda_init(l)
        mod = _modulation(c_all, w_ada[l].astype(BF16), b_ada[l][None, :]).reshape(B + Bs, 6, D_MODEL)
        w = {
            "n1g": norm1_g[l][None, :], "n2g": norm2_g[l][None, :],
            "w_in": w_in[l].astype(BF16),
            "gq": jnp.tile(q_norm_g[l], D_MODEL // HEAD_DIM)[None, :] * (HEAD_DIM ** -0.5 * LOG2E),
            "gk": jnp.tile(k_norm_g[l], D_MODEL // HEAD_DIM)[None, :],
            "conv_w": conv_w[l],
            "lam_vecs": jnp.stack([lambda_q1[l], lambda_k1[l], lambda_q2[l], lambda_k2[l]]),
            "subg": sub_norm_g[l][None, :],
            "wa": w_attn_out[l].astype(BF16), "wc": w_conv_out[l].astype(BF16), "wo": w_out[l].astype(BF16),
            "wgu": w_gate_up[l].astype(BF16), "wd": w_down[l].astype(BF16),
        }
        zeros_prev = jnp.zeros((B, CONV_WIDTH - 1, D_MODEL), xp.dtype)
        xp, kp, vp, cp = _layer(xp, mod[:B], None, None, zeros_prev, lam0, w, (512, 512))
        xs, ksn, vsn, csn = _layer(xs, mod[B:], cache_k[l].reshape(Bs, past, D_MODEL),
                                   cache_v[l].reshape(Bs, past, D_MODEL), state_conv[l], lam0, w, None)
        for lst, val in zip(outs, (kp, vp, cp, ksn, vsn, csn)):
            lst.append(val)
    return (xp, xs) + tuple(jnp.stack(o) for o in outs)
```

```python
import functools
import math

import jax
import jax.numpy as jnp
from jax import lax
from jax.experimental import pallas as pl
from jax.experimental.pallas import tpu as pltpu

D_MODEL = 1024
N_HEADS = 8
HEAD_DIM = 64
V_DIM = 2 * HEAD_DIM
CONV_WIDTH = 3
CHUNK = 64
D_FF = 2816
EPS = 1e-6
N_IN_GROUPS = 8
LOG2E = math.log2(math.e)
NEG_BIG = -1e30

LANES = 128
MXU_DIM = 256
VMEM_LIMIT = 56 * 1024 * 1024

F32 = jnp.float32
BF16 = jnp.bfloat16


def _lambda_init(layer_idx):
    return 0.8 - 0.6 * math.exp(-0.3 * layer_idx)


def _params(*sem):
    return pltpu.CompilerParams(dimension_semantics=sem, vmem_limit_bytes=VMEM_LIMIT)


def _resident(shape):
    return pl.BlockSpec(shape, lambda *_: (0,) * len(shape), pipeline_mode=pl.Buffered(1))


def _dot(a, b):
    return jnp.dot(a, b, preferred_element_type=F32)


def _dot_nt(a, b):
    return lax.dot_general(a, b, (((1,), (1,)), ((), ())), preferred_element_type=F32)


def _rms_rows(x):
    return x * lax.rsqrt(jnp.mean(x * x, axis=-1, keepdims=True) + EPS)


def _mod_kernel(c_ref, w_ref, b_ref, o_ref):
    c = c_ref[...]
    sc = (c * jax.nn.sigmoid(c)).astype(BF16)
    o_ref[...] = _dot(sc, w_ref[...]) + b_ref[...]


def _modulation(c_all, w_ada, b_ada):
    n = c_all.shape[0]
    return pl.pallas_call(
        _mod_kernel,
        grid=(6,),
        in_specs=[pl.BlockSpec((n, D_MODEL), lambda j: (0, 0)),
                  pl.BlockSpec((D_MODEL, D_MODEL), lambda j: (0, j)),
                  pl.BlockSpec((1, D_MODEL), lambda j: (0, j))],
        out_specs=pl.BlockSpec((n, D_MODEL), lambda j: (0, j)),
        out_shape=jax.ShapeDtypeStruct((n, 6 * D_MODEL), F32),
        compiler_params=_params("arbitrary"),
        name="adaln_mod",
    )(c_all, w_ada, b_ada)


def _group_mean_matrix():
    r = lax.broadcasted_iota(jnp.int32, (MXU_DIM, MXU_DIM), 0) // HEAD_DIM
    c = lax.broadcasted_iota(jnp.int32, (MXU_DIM, MXU_DIM), 1) // HEAD_DIM
    return jnp.where(r == c, 1.0 / HEAD_DIM, 0.0).astype(BF16)


def _group_rms(r, gmat):
    r2 = (r * r).astype(BF16)
    ms = jnp.concatenate(
        [_dot(r2[:, c * MXU_DIM:(c + 1) * MXU_DIM], gmat) for c in range(D_MODEL // MXU_DIM)], axis=1)
    return r * lax.rsqrt(ms + EPS)


def _inproj_kernel(x_ref, mod_ref, prev_ref, n1g_ref, w_ref, gq_ref, gk_ref, cw_ref,
                   qb_ref, kf_ref, kb_ref, vf_ref, vb_ref, yb_ref, sa_ref, sb_ref, tail_ref,
                   carry_ref, *, v_transposed):
    t = pl.program_id(1)
    tm = x_ref.shape[1]

    x = x_ref[0]
    h = (_rms_rows(x) * n1g_ref[...]) * (1.0 + mod_ref[0, 1:2, :]) + mod_ref[0, 0:1, :]
    h = h.astype(BF16)

    def proj(j):
        return _dot(h, w_ref[:, j * D_MODEL:(j + 1) * D_MODEL])

    gmat = _group_mean_matrix()

    qb_ref[0] = (_group_rms(proj(0), gmat) * gq_ref[...]).astype(BF16)

    kn = _group_rms(proj(1), gmat) * gk_ref[...]
    kf_ref[0] = kn
    kb_ref[0] = kn.astype(BF16)

    v = proj(2)
    vf_ref[0] = v
    vb_ref[0] = v.T.astype(BF16) if v_transposed else v.astype(BF16)

    @pl.when(t == 0)
    def _():
        carry_ref[0:2, :] = prev_ref[0]

    u = proj(5) * proj(3)
    p0 = carry_ref[0:1, :]
    p1 = carry_ref[1:2, :]
    row = lax.broadcasted_iota(jnp.int32, (tm, D_MODEL), 0)
    u1 = jnp.where(row == 0, p1, pltpu.roll(u, 1, 0))
    u2 = jnp.where(row == 0, p0, jnp.where(row == 1, p1, pltpu.roll(u, 2, 0)))
    cv = u2 * cw_ref[0:1, :] + u1 * cw_ref[1:2, :] + u * cw_ref[2:3, :]
    tail = u[tm - 2:tm, :]
    carry_ref[0:2, :] = tail
    tail_ref[0] = tail
    yb_ref[0] = (proj(4) * cv).astype(BF16)

    sa_ref[0] = jax.nn.sigmoid(proj(6)).astype(BF16)
    sb_ref[0] = jax.nn.sigmoid(proj(7)).astype(BF16)


def _in_projection(x, mod, conv_prev, n1g, w_in, gq, gk, conv_w, tm, v_transposed):
    B, T, _ = x.shape
    tile = pl.BlockSpec((1, tm, D_MODEL), lambda b, t: (b, t, 0))
    per_b2 = pl.BlockSpec((1, CONV_WIDTH - 1, D_MODEL), lambda b, t: (b, 0, 0))
    row = _resident((1, D_MODEL))
    act_bf = jax.ShapeDtypeStruct((B, T, D_MODEL), BF16)
    act_f32 = jax.ShapeDtypeStruct((B, T, D_MODEL), F32)
    if v_transposed:
        vb_spec = pl.BlockSpec((1, D_MODEL, tm), lambda b, t: (b, 0, t))
        vb_shape = jax.ShapeDtypeStruct((B, D_MODEL, T), BF16)
    else:
        vb_spec, vb_shape = tile, act_bf
    return pl.pallas_call(
        functools.partial(_inproj_kernel, v_transposed=v_transposed),
        grid=(B, T // tm),
        in_specs=[tile,
                  pl.BlockSpec((1, 6, D_MODEL), lambda b, t: (b, 0, 0)),
                  per_b2,
                  row,
                  _resident((D_MODEL, N_IN_GROUPS * D_MODEL)),
                  row, row,
                  _resident((CONV_WIDTH, D_MODEL))],
        out_specs=[tile] * 4 + [vb_spec] + [tile] * 3 + [per_b2],
        out_shape=[act_bf, act_f32, act_bf, act_f32, vb_shape, act_bf, act_bf, act_bf,
                   jax.ShapeDtypeStruct((B, CONV_WIDTH - 1, D_MODEL), F32)],
        scratch_shapes=[pltpu.VMEM((8, D_MODEL), F32)],
        compiler_params=_params("arbitrary", "arbitrary"),
        name="in_projection",
    )(x, mod, conv_prev, n1g, w_in, gq, gk, conv_w)


def _stack_maps(q):
    lane = lax.broadcasted_iota(jnp.int32, q.shape, 1)
    zero = jnp.zeros_like(q)
    return jnp.concatenate([jnp.where(lane < HEAD_DIM, q, zero), jnp.where(lane >= HEAD_DIM, q, zero)], axis=0)


def _lambda_full(lam_ref, lam0):
    lv = lam_ref[...]
    a = jnp.sum(lv[0:1, :] * lv[1:2, :], axis=-1, keepdims=True)
    b = jnp.sum(lv[2:3, :] * lv[3:4, :], axis=-1, keepdims=True)
    return jnp.exp(a) - jnp.exp(b) + lam0


def _finish_head(o1, o2, lam, subg, lam0):
    o = o1 - lam * o2
    return (_rms_rows(o) * subg) * (1.0 - lam0)


def _attn_prompt_kernel(q_ref, k_ref, vt_ref, lam_ref, subg_ref, o_ref,
                        m_scr, l_scr, acc_scr, sa_scr, sb_scr, *, lam0):
    qi = pl.program_id(2)
    tq = q_ref.shape[1]
    tk = tq
    cols = 2 * tq
    qs = _stack_maps(q_ref[0])

    m_scr[...] = jnp.full(m_scr.shape, NEG_BIG, F32)
    l_scr[...] = jnp.zeros(l_scr.shape, F32)
    acc_scr[...] = jnp.zeros(acc_scr.shape, F32)

    def scores(j, s_ref):
        k = k_ref[0, pl.ds(pl.multiple_of(j * tk, tk), tk), :]
        s_ref[...] = _dot_nt(k, qs)

    def update(j, s_ref, diagonal):
        vt = vt_ref[0, :, pl.ds(pl.multiple_of(j * tk, tk), tk)]
        s = s_ref[...]
        if diagonal:
            kchunk = lax.broadcasted_iota(jnp.int32, (tk, cols), 0) // CHUNK
            qchunk = (lax.broadcasted_iota(jnp.int32, (tk, cols), 1) % tq) // CHUNK
            s = jnp.where(kchunk <= qchunk, s, NEG_BIG)
        m_prev = m_scr[...]
        m_next = jnp.maximum(m_prev, jnp.max(s, axis=0, keepdims=True))
        alpha = jnp.exp2(m_prev - m_next)
        p = jnp.exp2(s - m_next)
        l_scr[...] = alpha * l_scr[...] + jnp.sum(p, axis=0, keepdims=True)
        acc_scr[...] = alpha * acc_scr[...] + _dot(vt, p.astype(BF16))
        m_scr[...] = m_next

    scores(0, sa_scr)

    def pair(i, carry):
        j = 2 * i
        scores(j + 1, sb_scr)
        update(j, sa_scr, False)
        scores(j + 2, sa_scr)
        update(j + 1, sb_scr, False)
        return carry

    lax.fori_loop(0, qi // 2, pair, 0)

    @pl.when(qi % 2 == 1)
    def _():
        scores(qi, sb_scr)
        update(qi - 1, sa_scr, False)
        update(qi, sb_scr, True)

    @pl.when(qi % 2 == 0)
    def _():
        update(qi, sa_scr, True)

    ot = acc_scr[...] * (1.0 / l_scr[...])
    lam = _lambda_full(lam_ref, lam0)
    o = ot[:, :tq] - lam * ot[:, tq:]
    o = (o * lax.rsqrt(jnp.mean(o * o, axis=0, keepdims=True) + EPS) * subg_ref[...]) * (1.0 - lam0)
    o_ref[0] = o.T.astype(o_ref.dtype)


def _attention_prompt(qb, kb, vt, lam_vecs, subg_col, lam0, tq):
    B, S, _ = qb.shape
    q_spec = pl.BlockSpec((1, tq, V_DIM), lambda b, h, i: (b, i, h))
    return pl.pallas_call(
        functools.partial(_attn_prompt_kernel, lam0=lam0),
        grid=(B, N_HEADS, S // tq),
        in_specs=[q_spec,
                  pl.BlockSpec((1, S, V_DIM), lambda b, h, i: (b, 0, h)),
                  pl.BlockSpec((1, V_DIM, S), lambda b, h, i: (b, h, 0)),
                  pl.BlockSpec((4, HEAD_DIM), lambda b, h, i: (0, 0)),
                  pl.BlockSpec((V_DIM, 1), lambda b, h, i: (0, 0))],
        out_specs=q_spec,
        out_shape=jax.ShapeDtypeStruct((B, S, D_MODEL), BF16),
        scratch_shapes=[pltpu.VMEM((1, 2 * tq), F32),
                        pltpu.VMEM((1, 2 * tq), F32),
                        pltpu.VMEM((V_DIM, 2 * tq), F32),
                        pltpu.VMEM((tq, 2 * tq), F32),
                        pltpu.VMEM((tq, 2 * tq), F32)],
        compiler_params=_params("arbitrary", "arbitrary", "arbitrary"),
        name="attn_prompt",
    )(qb, kb, vt, lam_vecs, subg_col)


def _attn_sample_kernel(q_ref, kn_ref, vn_ref, kc_ref, vc_ref, lam_ref, subg_ref, o_ref, *, lam0):
    tq = q_ref.shape[1]
    qs = _stack_maps(q_ref[0])
    s_c = _dot_nt(qs, kc_ref[0].astype(BF16))
    s_n = _dot_nt(qs, kn_ref[0])
    m = jnp.maximum(jnp.max(s_c, axis=1, keepdims=True), jnp.max(s_n, axis=1, keepdims=True))
    p_c = jnp.exp2(s_c - m)
    p_n = jnp.exp2(s_n - m)
    l = jnp.sum(p_c, axis=1, keepdims=True) + jnp.sum(p_n, axis=1, keepdims=True)
    acc = _dot(p_c.astype(BF16), vc_ref[0].astype(BF16)) + _dot(p_n.astype(BF16), vn_ref[0])
    o = acc / l
    lam = _lambda_full(lam_ref, lam0)
    o_ref[0] = _finish_head(o[:tq], o[tq:], lam, subg_ref[...], lam0).astype(o_ref.dtype)


def _attention_sample(qb, kb, vb, cache_k, cache_v, lam_vecs, subg, lam0):
    B, T, _ = qb.shape
    P = cache_k.shape[1]
    new_spec = pl.BlockSpec((1, T, V_DIM), lambda b, h: (b, 0, h))
    past_spec = pl.BlockSpec((1, P, V_DIM), lambda b, h: (b, 0, h))
    return pl.pallas_call(
        functools.partial(_attn_sample_kernel, lam0=lam0),
        grid=(B, N_HEADS),
        in_specs=[new_spec, new_spec, new_spec, past_spec, past_spec,
                  pl.BlockSpec((4, HEAD_DIM), lambda b, h: (0, 0)),
                  pl.BlockSpec((1, V_DIM), lambda b, h: (0, 0))],
        out_specs=new_spec,
        out_shape=jax.ShapeDtypeStruct((B, T, D_MODEL), BF16),
        compiler_params=_params("arbitrary", "arbitrary"),
        name="attn_sample",
    )(qb, kb, vb, cache_k, cache_v, lam_vecs, subg)


def _merge_kernel(x_ref, mod_ref, o_ref, yb_ref, sa_ref, sb_ref, wa_ref, wc_ref, wo_ref, x1_ref):
    y_a = _dot(o_ref[0], wa_ref[...])
    y_b = _dot(yb_ref[0], wc_ref[...])
    m = sa_ref[0].astype(F32) * y_a + sb_ref[0].astype(F32) * y_b
    x1_ref[0] = x_ref[0] + mod_ref[0, 2:3, :] * _dot(m.astype(BF16), wo_ref[...])


def _merge(x, mod, ob, yb, sa, sb, wa, wc, wo, tm):
    B, T, _ = x.shape
    tile = pl.BlockSpec((1, tm, D_MODEL), lambda b, t: (b, t, 0))
    wspec = _resident((D_MODEL, D_MODEL))
    return pl.pallas_call(
        _merge_kernel,
        grid=(B, T // tm),
        in_specs=[tile, pl.BlockSpec((1, 6, D_MODEL), lambda b, t: (b, 0, 0)),
                  tile, tile, tile, tile, wspec, wspec, wspec],
        out_specs=tile,
        out_shape=jax.ShapeDtypeStruct((B, T, D_MODEL), F32),
        compiler_params=_params("arbitrary", "arbitrary"),
        name="branch_merge",
    )(x, mod, ob, yb, sa, sb, wa, wc, wo)


def _ffn_chunks():
    step = 2 * MXU_DIM
    return [(c0, min(step, D_FF - c0)) for c0 in range(0, D_FF, step)]


def _ffn_kernel(x_ref, mod_ref, n2g_ref, wgu_ref, wd_ref, y_ref):
    x1 = x_ref[0]
    h2 = (_rms_rows(x1) * n2g_ref[...]) * (1.0 + mod_ref[0, 4:5, :]) + mod_ref[0, 3:4, :]
    h2 = h2.astype(BF16)
    acc = jnp.zeros(x1.shape, F32)
    for c0, cs in _ffn_chunks():
        g = _dot(h2, wgu_ref[:, c0:c0 + cs])
        up = _dot(h2, wgu_ref[:, D_FF + c0:D_FF + c0 + cs])
        a = ((g * jax.nn.sigmoid(g)) * up).astype(BF16)
        acc = acc + _dot(a, wd_ref[c0:c0 + cs, :])
    y_ref[0] = x1 + mod_ref[0, 5:6, :] * acc


def _ffn(x1, mod, n2g, wgu, wd, tm):
    B, T, _ = x1.shape
    tile = pl.BlockSpec((1, tm, D_MODEL), lambda b, t: (b, t, 0))
    return pl.pallas_call(
        _ffn_kernel,
        grid=(B, T // tm),
        in_specs=[tile, pl.BlockSpec((1, 6, D_MODEL), lambda b, t: (b, 0, 0)),
                  _resident((1, D_MODEL)),
                  _resident((D_MODEL, 2 * D_FF)),
                  _resident((D_FF, D_MODEL))],
        out_specs=tile,
        out_shape=jax.ShapeDtypeStruct((B, T, D_MODEL), F32),
        compiler_params=_params("arbitrary", "arbitrary"),
        name="swiglu_ffn",
    )(x1, mod, n2g, wgu, wd)


def _tile_rows(T):
    return min(T, 512)


def _layer(x, mod, past_k, past_v, conv_prev, lam0, w, attn_tq):
    B, T, _ = x.shape
    tm = _tile_rows(T)
    qb, kf, kb, vf, vb, yb, sa, sb, tail = _in_projection(
        x, mod, conv_prev, w["n1g"], w["w_in"], w["gq"], w["gk"], w["conv_w"], tm, past_k is None)
    if past_k is None:
        ob = _attention_prompt(qb, kb, vb, w["lam_vecs"], w["subg"].reshape(V_DIM, 1), lam0, attn_tq)
    else:
        ob = _attention_sample(qb, kb, vb, past_k, past_v, w["lam_vecs"], w["subg"], lam0)
    x1 = _merge(x, mod, ob, yb, sa, sb, w["wa"], w["wc"], w["wo"], tm)
    y = _ffn(x1, mod, w["n2g"], w["wgu"], w["wd"], tm)
    return (y, kf.reshape(B, T, N_HEADS, 2, HEAD_DIM), vf.reshape(B, T, N_HEADS, V_DIM), tail)


def kernel(x_prompt, x_sample, cache_k, cache_v, state_conv, c_prompt, c_sample, w_ada, b_ada, norm1_g, norm2_g, w_in, q_norm_g, k_norm_g, lambda_q1, lambda_k1, lambda_q2, lambda_k2, sub_norm_g, w_attn_out, conv_w, w_conv_out, w_out, w_gate_up, w_down):
    depth = w_ada.shape[0]
    B = x_prompt.shape[0]
    Bs, _, _ = x_sample.shape
    past = cache_k.shape[2]
    xp, xs = x_prompt, x_sample
    outs = [[] for _ in range(6)]
    c_all = jnp.concatenate([c_prompt, c_sample], axis=0)
    for l in range(depth):
        lam0 = _lambda_init(l)
        mod = _modulation(c_all, w_ada[l].astype(BF16), b_ada[l][None, :]).reshape(B + Bs, 6, D_MODEL)
        w = {
            "n1g": norm1_g[l][None, :], "n2g": norm2_g[l][None, :],
            "w_in": w_in[l].astype(BF16),
            "gq": jnp.tile(q_norm_g[l], D_MODEL // HEAD_DIM)[None, :] * (HEAD_DIM ** -0.5 * LOG2E),
            "gk": jnp.tile(k_norm_g[l], D_MODEL // HEAD_DIM)[None, :],
            "conv_w": conv_w[l],
            "lam_vecs": jnp.stack([lambda_q1[l], lambda_k1[l], lambda_q2[l], lambda_k2[l]]),
            "subg": sub_norm_g[l][None, :],
            "wa": w_attn_out[l].astype(BF16), "wc": w_conv_out[l].astype(BF16), "wo": w_out[l].astype(BF16),
            "wgu": w_gate_up[l].astype(BF16), "wd": w_down[l].astype(BF16),
        }
        zeros_prev = jnp.zeros((B, CONV_WIDTH - 1, D_MODEL), xp.dtype)
        xp, kp, vp, cp = _layer(xp, mod[:B], None, None, zeros_prev, lam0, w, 512)
        xs, ksn, vsn, csn = _layer(xs, mod[B:], cache_k[l].reshape(Bs, past, D_MODEL),
                                   cache_v[l].reshape(Bs, past, D_MODEL), state_conv[l], lam0, w, None)
        for lst, val in zip(outs, (kp, vp, cp, ksn, vsn, csn)):
            lst.append(val)
    return (xp, xs) + tuple(jnp.stack(o) for o in outs)
```

```python
import functools
import math

import jax
import jax.numpy as jnp
from jax import lax
from jax.experimental import pallas as pl
from jax.experimental.pallas import tpu as pltpu

D_MODEL = 1024
N_HEADS = 8
HEAD_DIM = 64
V_DIM = 2 * HEAD_DIM
CONV_WIDTH = 3
CHUNK = 64
D_FF = 2816
EPS = 1e-6
N_IN_GROUPS = 8
LOG2E = math.log2(math.e)
NEG_BIG = -1e30
SCORE_BOUND_MAX = 32.0

LANES = 128
MXU_DIM = 256
VMEM_LIMIT = 56 * 1024 * 1024

F32 = jnp.float32
BF16 = jnp.bfloat16


def _lambda_init(layer_idx):
    return 0.8 - 0.6 * math.exp(-0.3 * layer_idx)


def _params(*sem):
    return pltpu.CompilerParams(dimension_semantics=sem, vmem_limit_bytes=VMEM_LIMIT)


def _resident(shape):
    return pl.BlockSpec(shape, lambda *_: (0,) * len(shape), pipeline_mode=pl.Buffered(1))


def _dot(a, b):
    return jnp.dot(a, b, preferred_element_type=F32)


def _dot_nt(a, b):
    return lax.dot_general(a, b, (((1,), (1,)), ((), ())), preferred_element_type=F32)


def _rms_rows(x):
    return x * lax.rsqrt(jnp.mean(x * x, axis=-1, keepdims=True) + EPS)


def _mod_kernel(c_ref, w_ref, b_ref, o_ref):
    c = c_ref[...]
    sc = (c * jax.nn.sigmoid(c)).astype(BF16)
    o_ref[...] = _dot(sc, w_ref[...]) + b_ref[...]


def _modulation(c_all, w_ada, b_ada):
    n = c_all.shape[0]
    return pl.pallas_call(
        _mod_kernel,
        grid=(6,),
        in_specs=[pl.BlockSpec((n, D_MODEL), lambda j: (0, 0)),
                  pl.BlockSpec((D_MODEL, D_MODEL), lambda j: (0, j)),
                  pl.BlockSpec((1, D_MODEL), lambda j: (0, j))],
        out_specs=pl.BlockSpec((n, D_MODEL), lambda j: (0, j)),
        out_shape=jax.ShapeDtypeStruct((n, 6 * D_MODEL), F32),
        compiler_params=_params("arbitrary"),
        name="adaln_mod",
    )(c_all, w_ada, b_ada)


def _group_mean_matrix():
    r = lax.broadcasted_iota(jnp.int32, (MXU_DIM, MXU_DIM), 0) // HEAD_DIM
    c = lax.broadcasted_iota(jnp.int32, (MXU_DIM, MXU_DIM), 1) // HEAD_DIM
    return jnp.where(r == c, 1.0 / HEAD_DIM, 0.0).astype(BF16)


def _group_rms(r, gmat):
    r2 = (r * r).astype(BF16)
    ms = jnp.concatenate(
        [_dot(r2[:, c * MXU_DIM:(c + 1) * MXU_DIM], gmat) for c in range(D_MODEL // MXU_DIM)], axis=1)
    return r * lax.rsqrt(ms + EPS)


def _inproj_kernel(x_ref, mod_ref, prev_ref, n1g_ref, w_ref, gq_ref, gk_ref, cw_ref,
                   qb_ref, kf_ref, kb_ref, vf_ref, vb_ref, yb_ref, sa_ref, sb_ref, tail_ref,
                   carry_ref, *, v_transposed):
    t = pl.program_id(1)
    tm = x_ref.shape[1]

    x = x_ref[0]
    h = (_rms_rows(x) * n1g_ref[...]) * (1.0 + mod_ref[0, 1:2, :]) + mod_ref[0, 0:1, :]
    h = h.astype(BF16)

    def proj(j):
        return _dot(h, w_ref[:, j * D_MODEL:(j + 1) * D_MODEL])

    gmat = _group_mean_matrix()

    qb_ref[0] = (_group_rms(proj(0), gmat) * gq_ref[...]).astype(BF16)

    kn = _group_rms(proj(1), gmat) * gk_ref[...]
    kf_ref[0] = kn
    kb_ref[0] = kn.astype(BF16)

    v = proj(2)
    vf_ref[0] = v
    vb_ref[0] = v.T.astype(BF16) if v_transposed else v.astype(BF16)

    @pl.when(t == 0)
    def _():
        carry_ref[0:2, :] = prev_ref[0]

    u = proj(5) * proj(3)
    p0 = carry_ref[0:1, :]
    p1 = carry_ref[1:2, :]
    row = lax.broadcasted_iota(jnp.int32, (tm, D_MODEL), 0)
    u1 = jnp.where(row == 0, p1, pltpu.roll(u, 1, 0))
    u2 = jnp.where(row == 0, p0, jnp.where(row == 1, p1, pltpu.roll(u, 2, 0)))
    cv = u2 * cw_ref[0:1, :] + u1 * cw_ref[1:2, :] + u * cw_ref[2:3, :]
    tail = u[tm - 2:tm, :]
    carry_ref[0:2, :] = tail
    tail_ref[0] = tail
    yb_ref[0] = (proj(4) * cv).astype(BF16)

    sa_ref[0] = jax.nn.sigmoid(proj(6)).astype(BF16)
    sb_ref[0] = jax.nn.sigmoid(proj(7)).astype(BF16)


def _in_projection(x, mod, conv_prev, n1g, w_in, gq, gk, conv_w, tm, v_transposed):
    B, T, _ = x.shape
    tile = pl.BlockSpec((1, tm, D_MODEL), lambda b, t: (b, t, 0))
    per_b2 = pl.BlockSpec((1, CONV_WIDTH - 1, D_MODEL), lambda b, t: (b, 0, 0))
    row = _resident((1, D_MODEL))
    act_bf = jax.ShapeDtypeStruct((B, T, D_MODEL), BF16)
    act_f32 = jax.ShapeDtypeStruct((B, T, D_MODEL), F32)
    if v_transposed:
        vb_spec = pl.BlockSpec((1, D_MODEL, tm), lambda b, t: (b, 0, t))
        vb_shape = jax.ShapeDtypeStruct((B, D_MODEL, T), BF16)
    else:
        vb_spec, vb_shape = tile, act_bf
    return pl.pallas_call(
        functools.partial(_inproj_kernel, v_transposed=v_transposed),
        grid=(B, T // tm),
        in_specs=[tile,
                  pl.BlockSpec((1, 6, D_MODEL), lambda b, t: (b, 0, 0)),
                  per_b2,
                  row,
                  _resident((D_MODEL, N_IN_GROUPS * D_MODEL)),
                  row, row,
                  _resident((CONV_WIDTH, D_MODEL))],
        out_specs=[tile] * 4 + [vb_spec] + [tile] * 3 + [per_b2],
        out_shape=[act_bf, act_f32, act_bf, act_f32, vb_shape, act_bf, act_bf, act_bf,
                   jax.ShapeDtypeStruct((B, CONV_WIDTH - 1, D_MODEL), F32)],
        scratch_shapes=[pltpu.VMEM((8, D_MODEL), F32)],
        compiler_params=_params("arbitrary", "arbitrary"),
        name="in_projection",
    )(x, mod, conv_prev, n1g, w_in, gq, gk, conv_w)


def _stack_maps(q):
    lane = lax.broadcasted_iota(jnp.int32, q.shape, 1)
    zero = jnp.zeros_like(q)
    return jnp.concatenate([jnp.where(lane < HEAD_DIM, q, zero), jnp.where(lane >= HEAD_DIM, q, zero)], axis=0)


def _lambda_full(lam_ref, lam0):
    lv = lam_ref[...]
    a = jnp.sum(lv[0:1, :] * lv[1:2, :], axis=-1, keepdims=True)
    b = jnp.sum(lv[2:3, :] * lv[3:4, :], axis=-1, keepdims=True)
    return jnp.exp(a) - jnp.exp(b) + lam0


def _finish_head(o1, o2, lam, subg, lam0):
    o = o1 - lam * o2
    return (_rms_rows(o) * subg) * (1.0 - lam0)


def _attn_prompt_kernel(q_ref, k_ref, vt_ref, lam_ref, subg_ref, o_ref,
                        m_scr, l_scr, acc_scr, sa_scr, sb_scr, *, lam0, running_max):
    qi = pl.program_id(2)
    tq = q_ref.shape[1]
    tk = tq
    cols = 2 * tq
    qs = _stack_maps(q_ref[0])

    m_scr[...] = jnp.full(m_scr.shape, NEG_BIG, F32)
    l_scr[...] = jnp.zeros(l_scr.shape, F32)
    acc_scr[...] = jnp.zeros(acc_scr.shape, F32)

    def scores(j, s_ref):
        k = k_ref[0, pl.ds(pl.multiple_of(j * tk, tk), tk), :]
        s_ref[...] = _dot_nt(k, qs)

    def update(j, s_ref, diagonal):
        vt = vt_ref[0, :, pl.ds(pl.multiple_of(j * tk, tk), tk)]
        s = s_ref[...]
        if diagonal:
            kchunk = lax.broadcasted_iota(jnp.int32, (tk, cols), 0) // CHUNK
            qchunk = (lax.broadcasted_iota(jnp.int32, (tk, cols), 1) % tq) // CHUNK
            s = jnp.where(kchunk <= qchunk, s, NEG_BIG)
        if running_max:
            m_prev = m_scr[...]
            m_next = jnp.maximum(m_prev, jnp.max(s, axis=0, keepdims=True))
            alpha = jnp.exp2(m_prev - m_next)
            p = jnp.exp2(s - m_next)
            l_scr[...] = alpha * l_scr[...] + jnp.sum(p, axis=0, keepdims=True)
            acc_scr[...] = alpha * acc_scr[...] + _dot(vt, p.astype(BF16))
            m_scr[...] = m_next
        else:
            p = jnp.exp2(s)
            l_scr[...] += jnp.sum(p, axis=0, keepdims=True)
            acc_scr[...] += _dot(vt, p.astype(BF16))

    scores(0, sa_scr)

    def pair(j):
        scores(j + 1, sb_scr)
        update(j, sa_scr, False)
        scores(j + 2, sa_scr)
        update(j + 1, sb_scr, False)

    def quad(i, carry):
        pair(4 * i)
        pair(4 * i + 2)
        return carry

    lax.fori_loop(0, qi // 4, quad, 0)

    @pl.when(qi % 4 >= 2)
    def _():
        pair((qi // 4) * 4)

    @pl.when(qi % 2 == 1)
    def _():
        scores(qi, sb_scr)
        update(qi - 1, sa_scr, False)
        update(qi, sb_scr, True)

    @pl.when(qi % 2 == 0)
    def _():
        update(qi, sa_scr, True)

    ot = acc_scr[...] * (1.0 / l_scr[...])
    lam = _lambda_full(lam_ref, lam0)
    o = ot[:, :tq] - lam * ot[:, tq:]
    o = (o * lax.rsqrt(jnp.mean(o * o, axis=0, keepdims=True) + EPS) * subg_ref[...]) * (1.0 - lam0)
    o_ref[0] = o.T.astype(o_ref.dtype)


def _attention_prompt(qb, kb, vt, lam_vecs, subg_col, lam0, tq, running_max):
    B, S, _ = qb.shape
    q_spec = pl.BlockSpec((1, tq, V_DIM), lambda b, h, i: (b, i, h))
    return pl.pallas_call(
        functools.partial(_attn_prompt_kernel, lam0=lam0, running_max=running_max),
        grid=(B, N_HEADS, S // tq),
        in_specs=[q_spec,
                  pl.BlockSpec((1, S, V_DIM), lambda b, h, i: (b, 0, h)),
                  pl.BlockSpec((1, V_DIM, S), lambda b, h, i: (b, h, 0)),
                  pl.BlockSpec((4, HEAD_DIM), lambda b, h, i: (0, 0)),
                  pl.BlockSpec((V_DIM, 1), lambda b, h, i: (0, 0))],
        out_specs=q_spec,
        out_shape=jax.ShapeDtypeStruct((B, S, D_MODEL), BF16),
        scratch_shapes=[pltpu.VMEM((1, 2 * tq), F32),
                        pltpu.VMEM((1, 2 * tq), F32),
                        pltpu.VMEM((V_DIM, 2 * tq), F32),
                        pltpu.VMEM((tq, 2 * tq), F32),
                        pltpu.VMEM((tq, 2 * tq), F32)],
        compiler_params=_params("arbitrary", "arbitrary", "arbitrary"),
        name="attn_prompt_running_max" if running_max else "attn_prompt_direct",
    )(qb, kb, vt, lam_vecs, subg_col)


def _attn_sample_kernel(q_ref, kn_ref, vn_ref, kc_ref, vc_ref, lam_ref, subg_ref, o_ref, *, lam0):
    tq = q_ref.shape[1]
    qs = _stack_maps(q_ref[0])
    s_c = _dot_nt(qs, kc_ref[0].astype(BF16))
    s_n = _dot_nt(qs, kn_ref[0])
    m = jnp.maximum(jnp.max(s_c, axis=1, keepdims=True), jnp.max(s_n, axis=1, keepdims=True))
    p_c = jnp.exp2(s_c - m)
    p_n = jnp.exp2(s_n - m)
    l = jnp.sum(p_c, axis=1, keepdims=True) + jnp.sum(p_n, axis=1, keepdims=True)
    acc = _dot(p_c.astype(BF16), vc_ref[0].astype(BF16)) + _dot(p_n.astype(BF16), vn_ref[0])
    o = acc / l
    lam = _lambda_full(lam_ref, lam0)
    o_ref[0] = _finish_head(o[:tq], o[tq:], lam, subg_ref[...], lam0).astype(o_ref.dtype)


def _attention_sample(qb, kb, vb, cache_k, cache_v, lam_vecs, subg, lam0):
    B, T, _ = qb.shape
    P = cache_k.shape[1]
    new_spec = pl.BlockSpec((1, T, V_DIM), lambda b, h: (b, 0, h))
    past_spec = pl.BlockSpec((1, P, V_DIM), lambda b, h: (b, 0, h))
    return pl.pallas_call(
        functools.partial(_attn_sample_kernel, lam0=lam0),
        grid=(B, N_HEADS),
        in_specs=[new_spec, new_spec, new_spec, past_spec, past_spec,
                  pl.BlockSpec((4, HEAD_DIM), lambda b, h: (0, 0)),
                  pl.BlockSpec((1, V_DIM), lambda b, h: (0, 0))],
        out_specs=new_spec,
        out_shape=jax.ShapeDtypeStruct((B, T, D_MODEL), BF16),
        compiler_params=_params("arbitrary", "arbitrary"),
        name="attn_sample",
    )(qb, kb, vb, cache_k, cache_v, lam_vecs, subg)


def _merge_kernel(x_ref, mod_ref, o_ref, yb_ref, sa_ref, sb_ref, wa_ref, wc_ref, wo_ref, x1_ref):
    y_a = _dot(o_ref[0], wa_ref[...])
    y_b = _dot(yb_ref[0], wc_ref[...])
    m = sa_ref[0].astype(F32) * y_a + sb_ref[0].astype(F32) * y_b
    x1_ref[0] = x_ref[0] + mod_ref[0, 2:3, :] * _dot(m.astype(BF16), wo_ref[...])


def _merge(x, mod, ob, yb, sa, sb, wa, wc, wo, tm):
    B, T, _ = x.shape
    tile = pl.BlockSpec((1, tm, D_MODEL), lambda b, t: (b, t, 0))
    wspec = _resident((D_MODEL, D_MODEL))
    return pl.pallas_call(
        _merge_kernel,
        grid=(B, T // tm),
        in_specs=[tile, pl.BlockSpec((1, 6, D_MODEL), lambda b, t: (b, 0, 0)),
                  tile, tile, tile, tile, wspec, wspec, wspec],
        out_specs=tile,
        out_shape=jax.ShapeDtypeStruct((B, T, D_MODEL), F32),
        compiler_params=_params("arbitrary", "arbitrary"),
        name="branch_merge",
    )(x, mod, ob, yb, sa, sb, wa, wc, wo)


def _ffn_chunks():
    step = 2 * MXU_DIM
    return [(c0, min(step, D_FF - c0)) for c0 in range(0, D_FF, step)]


def _ffn_kernel(x_ref, mod_ref, n2g_ref, wgu_ref, wd_ref, y_ref):
    x1 = x_ref[0]
    h2 = (_rms_rows(x1) * n2g_ref[...]) * (1.0 + mod_ref[0, 4:5, :]) + mod_ref[0, 3:4, :]
    h2 = h2.astype(BF16)
    acc = jnp.zeros(x1.shape, F32)
    for c0, cs in _ffn_chunks():
        g = _dot(h2, wgu_ref[:, c0:c0 + cs])
        up = _dot(h2, wgu_ref[:, D_FF + c0:D_FF + c0 + cs])
        a = ((g * jax.nn.sigmoid(g)) * up).astype(BF16)
        acc = acc + _dot(a, wd_ref[c0:c0 + cs, :])
    y_ref[0] = x1 + mod_ref[0, 5:6, :] * acc


def _ffn(x1, mod, n2g, wgu, wd, tm):
    B, T, _ = x1.shape
    tile = pl.BlockSpec((1, tm, D_MODEL), lambda b, t: (b, t, 0))
    return pl.pallas_call(
        _ffn_kernel,
        grid=(B, T // tm),
        in_specs=[tile, pl.BlockSpec((1, 6, D_MODEL), lambda b, t: (b, 0, 0)),
                  _resident((1, D_MODEL)),
                  _resident((D_MODEL, 2 * D_FF)),
                  _resident((D_FF, D_MODEL))],
        out_specs=tile,
        out_shape=jax.ShapeDtypeStruct((B, T, D_MODEL), F32),
        compiler_params=_params("arbitrary", "arbitrary"),
        name="swiglu_ffn",
    )(x1, mod, n2g, wgu, wd)


def _tile_rows(T):
    return min(T, 512)


def _layer(x, mod, past_k, past_v, conv_prev, lam0, w, attn_tq):
    B, T, _ = x.shape
    tm = _tile_rows(T)
    qb, kf, kb, vf, vb, yb, sa, sb, tail = _in_projection(
        x, mod, conv_prev, w["n1g"], w["w_in"], w["gq"], w["gk"], w["conv_w"], tm, past_k is None)
    if past_k is None:
        score_bound = HEAD_DIM * jnp.max(jnp.abs(w["gq"])) * jnp.max(jnp.abs(w["gk"]))
        attend = functools.partial(_attention_prompt, qb, kb, vb, w["lam_vecs"], w["subg"].reshape(V_DIM, 1),
                                   lam0, attn_tq)
        ob = lax.cond(score_bound <= SCORE_BOUND_MAX, lambda: attend(False), lambda: attend(True))
    else:
        ob = _attention_sample(qb, kb, vb, past_k, past_v, w["lam_vecs"], w["subg"], lam0)
    x1 = _merge(x, mod, ob, yb, sa, sb, w["wa"], w["wc"], w["wo"], tm)
    y = _ffn(x1, mod, w["n2g"], w["wgu"], w["wd"], tm)
    return (y, kf.reshape(B, T, N_HEADS, 2, HEAD_DIM), vf.reshape(B, T, N_HEADS, V_DIM), tail)


def kernel(x_prompt, x_sample, cache_k, cache_v, state_conv, c_prompt, c_sample, w_ada, b_ada, norm1_g, norm2_g, w_in, q_norm_g, k_norm_g, lambda_q1, lambda_k1, lambda_q2, lambda_k2, sub_norm_g, w_attn_out, conv_w, w_conv_out, w_out, w_gate_up, w_down):
    depth = w_ada.shape[0]
    B = x_prompt.shape[0]
    Bs, _, _ = x_sample.shape
    past = cache_k.shape[2]
    xp, xs = x_prompt, x_sample
    outs = [[] for _ in range(6)]
    c_all = jnp.concatenate([c_prompt, c_sample], axis=0)
    for l in range(depth):
        lam0 = _lambda_init(l)
        mod = _modulation(c_all, w_ada[l].astype(BF16), b_ada[l][None, :]).reshape(B + Bs, 6, D_MODEL)
        w = {
            "n1g": norm1_g[l][None, :], "n2g": norm2_g[l][None, :],
            "w_in": w_in[l].astype(BF16),
            "gq": jnp.tile(q_norm_g[l], D_MODEL // HEAD_DIM)[None, :] * (HEAD_DIM ** -0.5 * LOG2E),
            "gk": jnp.tile(k_norm_g[l], D_MODEL // HEAD_DIM)[None, :],
            "conv_w": conv_w[l],
            "lam_vecs": jnp.stack([lambda_q1[l], lambda_k1[l], lambda_q2[l], lambda_k2[l]]),
            "subg": sub_norm_g[l][None, :],
            "wa": w_attn_out[l].astype(BF16), "wc": w_conv_out[l].astype(BF16), "wo": w_out[l].astype(BF16),
            "wgu": w_gate_up[l].astype(BF16), "wd": w_down[l].astype(BF16),
        }
        zeros_prev = jnp.zeros((B, CONV_WIDTH - 1, D_MODEL), xp.dtype)
        xp, kp, vp, cp = _layer(xp, mod[:B], None, None, zeros_prev, lam0, w, 512)
        xs, ksn, vsn, csn = _layer(xs, mod[B:], cache_k[l].reshape(Bs, past, D_MODEL),
                                   cache_v[l].reshape(Bs, past, D_MODEL), state_conv[l], lam0, w, None)
        for lst, val in zip(outs, (kp, vp, cp, ksn, vsn, csn)):
            lst.append(val)
    return (xp, xs) + tuple(jnp.stack(o) for o in outs)
```

```python
import functools
import math

import jax
import jax.numpy as jnp
from jax import lax
from jax.experimental import pallas as pl
from jax.experimental.pallas import tpu as pltpu

D_MODEL = 1024
N_HEADS = 8
HEAD_DIM = 64
V_DIM = 2 * HEAD_DIM
CONV_WIDTH = 3
CHUNK = 64
D_FF = 2816
EPS = 1e-6
N_IN_GROUPS = 8
LOG2E = math.log2(math.e)
NEG_BIG = -1e30
SCORE_BOUND_MAX = 32.0

LANES = 128
MXU_DIM = 256
VMEM_LIMIT = 56 * 1024 * 1024

F32 = jnp.float32
BF16 = jnp.bfloat16


def _lambda_init(layer_idx):
    return 0.8 - 0.6 * math.exp(-0.3 * layer_idx)


def _params(*sem):
    return pltpu.CompilerParams(dimension_semantics=sem, vmem_limit_bytes=VMEM_LIMIT)


def _resident(shape):
    return pl.BlockSpec(shape, lambda *_: (0,) * len(shape), pipeline_mode=pl.Buffered(1))


def _dot(a, b):
    return jnp.dot(a, b, preferred_element_type=F32)


def _dot_nt(a, b):
    return lax.dot_general(a, b, (((1,), (1,)), ((), ())), preferred_element_type=F32)


def _rms_rows(x):
    return x * lax.rsqrt(jnp.mean(x * x, axis=-1, keepdims=True) + EPS)


def _mod_kernel(c_ref, w_ref, b_ref, o_ref):
    c = c_ref[...]
    sc = (c * jax.nn.sigmoid(c)).astype(BF16)
    o_ref[...] = _dot(sc, w_ref[...]) + b_ref[...]


def _modulation(c_all, w_ada, b_ada):
    n = c_all.shape[0]
    return pl.pallas_call(
        _mod_kernel,
        grid=(6,),
        in_specs=[pl.BlockSpec((n, D_MODEL), lambda j: (0, 0)),
                  pl.BlockSpec((D_MODEL, D_MODEL), lambda j: (0, j)),
                  pl.BlockSpec((1, D_MODEL), lambda j: (0, j))],
        out_specs=pl.BlockSpec((n, D_MODEL), lambda j: (0, j)),
        out_shape=jax.ShapeDtypeStruct((n, 6 * D_MODEL), F32),
        compiler_params=_params("arbitrary"),
        name="adaln_mod",
    )(c_all, w_ada, b_ada)


def _group_mean_matrix():
    r = lax.broadcasted_iota(jnp.int32, (MXU_DIM, MXU_DIM), 0) // HEAD_DIM
    c = lax.broadcasted_iota(jnp.int32, (MXU_DIM, MXU_DIM), 1) // HEAD_DIM
    return jnp.where(r == c, 1.0 / HEAD_DIM, 0.0).astype(BF16)


def _group_rms(r, gmat):
    r2 = (r * r).astype(BF16)
    ms = jnp.concatenate(
        [_dot(r2[:, c * MXU_DIM:(c + 1) * MXU_DIM], gmat) for c in range(D_MODEL // MXU_DIM)], axis=1)
    return r * lax.rsqrt(ms + EPS)


def _inproj_kernel(x_ref, mod_ref, prev_ref, n1g_ref, w_ref, gq_ref, gk_ref, cw_ref,
                   qb_ref, kf_ref, kb_ref, vf_ref, vb_ref, yb_ref, sa_ref, sb_ref, tail_ref,
                   carry_ref, *, v_transposed):
    t = pl.program_id(1)
    tm = x_ref.shape[1]

    x = x_ref[0]
    h = (_rms_rows(x) * n1g_ref[...]) * (1.0 + mod_ref[0, 1:2, :]) + mod_ref[0, 0:1, :]
    h = h.astype(BF16)

    def proj(j):
        return _dot(h, w_ref[:, j * D_MODEL:(j + 1) * D_MODEL])

    gmat = _group_mean_matrix()

    qb_ref[0] = (_group_rms(proj(0), gmat) * gq_ref[...]).astype(BF16)

    kn = _group_rms(proj(1), gmat) * gk_ref[...]
    kf_ref[0] = kn
    kb_ref[0] = kn.astype(BF16)

    v = proj(2)
    vf_ref[0] = v
    vb_ref[0] = v.T.astype(BF16) if v_transposed else v.astype(BF16)

    @pl.when(t == 0)
    def _():
        carry_ref[0:2, :] = prev_ref[0]

    u = proj(5) * proj(3)
    p0 = carry_ref[0:1, :]
    p1 = carry_ref[1:2, :]
    row = lax.broadcasted_iota(jnp.int32, (tm, D_MODEL), 0)
    u1 = jnp.where(row == 0, p1, pltpu.roll(u, 1, 0))
    u2 = jnp.where(row == 0, p0, jnp.where(row == 1, p1, pltpu.roll(u, 2, 0)))
    cv = u2 * cw_ref[0:1, :] + u1 * cw_ref[1:2, :] + u * cw_ref[2:3, :]
    tail = u[tm - 2:tm, :]
    carry_ref[0:2, :] = tail
    tail_ref[0] = tail
    yb_ref[0] = (proj(4) * cv).astype(BF16)

    sa_ref[0] = jax.nn.sigmoid(proj(6)).astype(BF16)
    sb_ref[0] = jax.nn.sigmoid(proj(7)).astype(BF16)


def _in_projection(x, mod, conv_prev, n1g, w_in, gq, gk, conv_w, tm, v_transposed):
    B, T, _ = x.shape
    tile = pl.BlockSpec((1, tm, D_MODEL), lambda b, t: (b, t, 0))
    per_b2 = pl.BlockSpec((1, CONV_WIDTH - 1, D_MODEL), lambda b, t: (b, 0, 0))
    row = _resident((1, D_MODEL))
    act_bf = jax.ShapeDtypeStruct((B, T, D_MODEL), BF16)
    act_f32 = jax.ShapeDtypeStruct((B, T, D_MODEL), F32)
    if v_transposed:
        vb_spec = pl.BlockSpec((1, D_MODEL, tm), lambda b, t: (b, 0, t))
        vb_shape = jax.ShapeDtypeStruct((B, D_MODEL, T), BF16)
    else:
        vb_spec, vb_shape = tile, act_bf
    return pl.pallas_call(
        functools.partial(_inproj_kernel, v_transposed=v_transposed),
        grid=(B, T // tm),
        in_specs=[tile,
                  pl.BlockSpec((1, 6, D_MODEL), lambda b, t: (b, 0, 0)),
                  per_b2,
                  row,
                  _resident((D_MODEL, N_IN_GROUPS * D_MODEL)),
                  row, row,
                  _resident((CONV_WIDTH, D_MODEL))],
        out_specs=[tile] * 4 + [vb_spec] + [tile] * 3 + [per_b2],
        out_shape=[act_bf, act_f32, act_bf, act_f32, vb_shape, act_bf, act_bf, act_bf,
                   jax.ShapeDtypeStruct((B, CONV_WIDTH - 1, D_MODEL), F32)],
        scratch_shapes=[pltpu.VMEM((8, D_MODEL), F32)],
        compiler_params=_params("arbitrary", "arbitrary"),
        name="in_projection",
    )(x, mod, conv_prev, n1g, w_in, gq, gk, conv_w)


def _stack_maps(q):
    lane = lax.broadcasted_iota(jnp.int32, q.shape, 1)
    zero = jnp.zeros_like(q)
    return jnp.concatenate([jnp.where(lane < HEAD_DIM, q, zero), jnp.where(lane >= HEAD_DIM, q, zero)], axis=0)


def _lambda_full(lam_ref, lam0):
    lv = lam_ref[...]
    a = jnp.sum(lv[0:1, :] * lv[1:2, :], axis=-1, keepdims=True)
    b = jnp.sum(lv[2:3, :] * lv[3:4, :], axis=-1, keepdims=True)
    return jnp.exp(a) - jnp.exp(b) + lam0


def _finish_head(o1, o2, lam, subg, lam0):
    o = o1 - lam * o2
    return (_rms_rows(o) * subg) * (1.0 - lam0)


def _attn_prompt_kernel(qa_ref, qb_ref, k_ref, vt_ref, lam_ref, subg_ref, olo_ref, ohi_ref,
                        qs_scr, bias_scr, m_scr, l_scr, acc_scr, s0_scr, s1_scr, *, lam0, running_max):
    i = pl.program_id(2)
    tq = qa_ref.shape[1]
    tk = tq
    cols = 2 * tq
    n_tiles = k_ref.shape[1] // tq
    n_slots = n_tiles + 1
    half = n_tiles // 2

    qs_scr[0] = _stack_maps(qa_ref[0])
    qs_scr[1] = _stack_maps(qb_ref[0])
    kchunk = lax.broadcasted_iota(jnp.int32, (tk, cols), 0) // CHUNK
    qchunk = (lax.broadcasted_iota(jnp.int32, (tk, cols), 1) % tq) // CHUNK
    bias_scr[...] = jnp.where(kchunk <= qchunk, 0.0, NEG_BIG)
    m_scr[...] = jnp.full(m_scr.shape, NEG_BIG, F32)
    l_scr[...] = jnp.zeros(l_scr.shape, F32)
    acc_scr[...] = jnp.zeros(acc_scr.shape, F32)
    s_bufs = (s0_scr, s1_scr)

    def slot(t):
        if t >= half:
            return 1, t - i - 1, (1.0 if t == n_slots - 1 else None)
        is_a = t <= i
        return (jnp.where(is_a, 0, 1), jnp.where(is_a, t, t - i - 1),
                jnp.where(t == i, 1.0, 0.0).astype(F32))

    def scores(t):
        sel, kt, _ = slot(t)
        k = k_ref[0, pl.ds(pl.multiple_of(kt * tk, tk), tk), :]
        s_bufs[t % 2][...] = _dot_nt(k, qs_scr[sel])

    def update(t):
        sel, kt, diag = slot(t)
        vt = vt_ref[0, :, pl.ds(pl.multiple_of(kt * tk, tk), tk)]
        s = s_bufs[t % 2][...]
        if diag is not None:
            s = s + bias_scr[...] * diag
        if running_max:
            m_prev = m_scr[sel]
            m_next = jnp.maximum(m_prev, jnp.max(s, axis=0, keepdims=True))
            alpha = jnp.exp2(m_prev - m_next)
            p = jnp.exp2(s - m_next)
            l_scr[sel] = alpha * l_scr[sel] + jnp.sum(p, axis=0, keepdims=True)
            acc_scr[sel] = alpha * acc_scr[sel] + _dot(vt, p.astype(BF16))
            m_scr[sel] = m_next
        else:
            p = jnp.exp2(s)
            l_scr[sel] += jnp.sum(p, axis=0, keepdims=True)
            acc_scr[sel] += _dot(vt, p.astype(BF16))

    scores(0)
    for t in range(n_slots):
        if t + 1 < n_slots:
            scores(t + 1)
        update(t)

    lam = _lambda_full(lam_ref, lam0)
    for sel, o_ref in ((0, olo_ref), (1, ohi_ref)):
        ot = acc_scr[sel] * (1.0 / l_scr[sel])
        o = ot[:, :tq] - lam * ot[:, tq:]
        o = (o * lax.rsqrt(jnp.mean(o * o, axis=0, keepdims=True) + EPS) * subg_ref[...]) * (1.0 - lam0)
        o_ref[0] = o.T.astype(o_ref.dtype)


def _attention_prompt(qb, kb, vt, lam_vecs, subg_col, lam0, tq, running_max):
    B, S, _ = qb.shape
    n_tiles = S // tq
    half = n_tiles // 2
    tile = (1, tq, V_DIM)
    return pl.pallas_call(
        functools.partial(_attn_prompt_kernel, lam0=lam0, running_max=running_max),
        grid=(B, N_HEADS, half),
        in_specs=[pl.BlockSpec(tile, lambda b, h, i: (b, i, h)),
                  pl.BlockSpec(tile, lambda b, h, i: (b, n_tiles - 1 - i, h)),
                  pl.BlockSpec((1, S, V_DIM), lambda b, h, i: (b, 0, h)),
                  pl.BlockSpec((1, V_DIM, S), lambda b, h, i: (b, h, 0)),
                  pl.BlockSpec((4, HEAD_DIM), lambda b, h, i: (0, 0)),
                  pl.BlockSpec((V_DIM, 1), lambda b, h, i: (0, 0))],
        out_specs=[pl.BlockSpec(tile, lambda b, h, i: (b, i, h)),
                   pl.BlockSpec(tile, lambda b, h, i: (b, half - 1 - i, h))],
        out_shape=[jax.ShapeDtypeStruct((B, S // 2, D_MODEL), BF16)] * 2,
        scratch_shapes=[pltpu.VMEM((2, 2 * tq, V_DIM), BF16),
                        pltpu.VMEM((tq, 2 * tq), F32),
                        pltpu.VMEM((2, 1, 2 * tq), F32),
                        pltpu.VMEM((2, 1, 2 * tq), F32),
                        pltpu.VMEM((2, V_DIM, 2 * tq), F32),
                        pltpu.VMEM((tq, 2 * tq), F32),
                        pltpu.VMEM((tq, 2 * tq), F32)],
        compiler_params=_params("arbitrary", "arbitrary", "arbitrary"),
        name="attn_prompt_running_max" if running_max else "attn_prompt_direct",
    )(qb, qb, kb, vt, lam_vecs, subg_col)


def _attn_sample_kernel(q_ref, kn_ref, vn_ref, kc_ref, vc_ref, lam_ref, subg_ref, o_ref, *, lam0):
    tq = q_ref.shape[1]
    qs = _stack_maps(q_ref[0])
    s_c = _dot_nt(qs, kc_ref[0].astype(BF16))
    s_n = _dot_nt(qs, kn_ref[0])
    m = jnp.maximum(jnp.max(s_c, axis=1, keepdims=True), jnp.max(s_n, axis=1, keepdims=True))
    p_c = jnp.exp2(s_c - m)
    p_n = jnp.exp2(s_n - m)
    l = jnp.sum(p_c, axis=1, keepdims=True) + jnp.sum(p_n, axis=1, keepdims=True)
    acc = _dot(p_c.astype(BF16), vc_ref[0].astype(BF16)) + _dot(p_n.astype(BF16), vn_ref[0])
    o = acc / l
    lam = _lambda_full(lam_ref, lam0)
    o_ref[0] = _finish_head(o[:tq], o[tq:], lam, subg_ref[...], lam0).astype(o_ref.dtype)


def _attention_sample(qb, kb, vb, cache_k, cache_v, lam_vecs, subg, lam0):
    B, T, _ = qb.shape
    P = cache_k.shape[1]
    new_spec = pl.BlockSpec((1, T, V_DIM), lambda b, h: (b, 0, h))
    past_spec = pl.BlockSpec((1, P, V_DIM), lambda b, h: (b, 0, h))
    return pl.pallas_call(
        functools.partial(_attn_sample_kernel, lam0=lam0),
        grid=(B, N_HEADS),
        in_specs=[new_spec, new_spec, new_spec, past_spec, past_spec,
                  pl.BlockSpec((4, HEAD_DIM), lambda b, h: (0, 0)),
                  pl.BlockSpec((1, V_DIM), lambda b, h: (0, 0))],
        out_specs=new_spec,
        out_shape=jax.ShapeDtypeStruct((B, T, D_MODEL), BF16),
        compiler_params=_params("arbitrary", "arbitrary"),
        name="attn_sample",
    )(qb, kb, vb, cache_k, cache_v, lam_vecs, subg)


def _merge_kernel(x_ref, mod_ref, olo_ref, ohi_ref, yb_ref, sa_ref, sb_ref, wa_ref, wc_ref, wo_ref, x1_ref,
                  *, n_lo):
    o = jnp.where(pl.program_id(1) < n_lo, olo_ref[0], ohi_ref[0])
    y_a = _dot(o, wa_ref[...])
    y_b = _dot(yb_ref[0], wc_ref[...])
    m = sa_ref[0].astype(F32) * y_a + sb_ref[0].astype(F32) * y_b
    x1_ref[0] = x_ref[0] + mod_ref[0, 2:3, :] * _dot(m.astype(BF16), wo_ref[...])


def _merge(x, mod, o_lo, o_hi, yb, sa, sb, wa, wc, wo, tm):
    B, T, _ = x.shape
    n_lo = o_lo.shape[1] // tm
    tile = pl.BlockSpec((1, tm, D_MODEL), lambda b, t: (b, t, 0))
    wspec = _resident((D_MODEL, D_MODEL))
    return pl.pallas_call(
        functools.partial(_merge_kernel, n_lo=n_lo),
        grid=(B, T // tm),
        in_specs=[tile, pl.BlockSpec((1, 6, D_MODEL), lambda b, t: (b, 0, 0)),
                  pl.BlockSpec((1, tm, D_MODEL), lambda b, t: (b, jnp.minimum(t, n_lo - 1), 0)),
                  pl.BlockSpec((1, tm, D_MODEL), lambda b, t: (b, jnp.maximum(t - n_lo, 0), 0)),
                  tile, tile, tile, wspec, wspec, wspec],
        out_specs=tile,
        out_shape=jax.ShapeDtypeStruct((B, T, D_MODEL), F32),
        compiler_params=_params("arbitrary", "arbitrary"),
        name="branch_merge",
    )(x, mod, o_lo, o_hi, yb, sa, sb, wa, wc, wo)


def _ffn_chunks():
    step = 2 * MXU_DIM
    return [(c0, min(step, D_FF - c0)) for c0 in range(0, D_FF, step)]


def _ffn_kernel(x_ref, mod_ref, n2g_ref, wgu_ref, wd_ref, y_ref):
    x1 = x_ref[0]
    h2 = (_rms_rows(x1) * n2g_ref[...]) * (1.0 + mod_ref[0, 4:5, :]) + mod_ref[0, 3:4, :]
    h2 = h2.astype(BF16)
    acc = jnp.zeros(x1.shape, F32)
    for c0, cs in _ffn_chunks():
        g = _dot(h2, wgu_ref[:, c0:c0 + cs])
        up = _dot(h2, wgu_ref[:, D_FF + c0:D_FF + c0 + cs])
        a = ((g * jax.nn.sigmoid(g)) * up).astype(BF16)
        acc = acc + _dot(a, wd_ref[c0:c0 + cs, :])
    y_ref[0] = x1 + mod_ref[0, 5:6, :] * acc


def _ffn(x1, mod, n2g, wgu, wd, tm):
    B, T, _ = x1.shape
    tile = pl.BlockSpec((1, tm, D_MODEL), lambda b, t: (b, t, 0))
    return pl.pallas_call(
        _ffn_kernel,
        grid=(B, T // tm),
        in_specs=[tile, pl.BlockSpec((1, 6, D_MODEL), lambda b, t: (b, 0, 0)),
                  _resident((1, D_MODEL)),
                  _resident((D_MODEL, 2 * D_FF)),
                  _resident((D_FF, D_MODEL))],
        out_specs=tile,
        out_shape=jax.ShapeDtypeStruct((B, T, D_MODEL), F32),
        compiler_params=_params("arbitrary", "arbitrary"),
        name="swiglu_ffn",
    )(x1, mod, n2g, wgu, wd)


def _tile_rows(T):
    return min(T, 512)


def _layer(x, mod, past_k, past_v, conv_prev, lam0, w, attn_tq):
    B, T, _ = x.shape
    tm = _tile_rows(T)
    qb, kf, kb, vf, vb, yb, sa, sb, tail = _in_projection(
        x, mod, conv_prev, w["n1g"], w["w_in"], w["gq"], w["gk"], w["conv_w"], tm, past_k is None)
    if past_k is None:
        score_bound = HEAD_DIM * jnp.max(jnp.abs(w["gq"])) * jnp.max(jnp.abs(w["gk"]))
        attend = functools.partial(_attention_prompt, qb, kb, vb, w["lam_vecs"], w["subg"].reshape(V_DIM, 1),
                                   lam0, attn_tq)
        o_lo, o_hi = lax.cond(score_bound <= SCORE_BOUND_MAX, lambda: attend(False), lambda: attend(True))
    else:
        o_lo = o_hi = _attention_sample(qb, kb, vb, past_k, past_v, w["lam_vecs"], w["subg"], lam0)
    x1 = _merge(x, mod, o_lo, o_hi, yb, sa, sb, w["wa"], w["wc"], w["wo"], tm)
    y = _ffn(x1, mod, w["n2g"], w["wgu"], w["wd"], tm)
    return (y, kf.reshape(B, T, N_HEADS, 2, HEAD_DIM), vf.reshape(B, T, N_HEADS, V_DIM), tail)


def kernel(x_prompt, x_sample, cache_k, cache_v, state_conv, c_prompt, c_sample, w_ada, b_ada, norm1_g, norm2_g, w_in, q_norm_g, k_norm_g, lambda_q1, lambda_k1, lambda_q2, lambda_k2, sub_norm_g, w_attn_out, conv_w, w_conv_out, w_out, w_gate_up, w_down):
    depth = w_ada.shape[0]
    B = x_prompt.shape[0]
    Bs, _, _ = x_sample.shape
    past = cache_k.shape[2]
    xp, xs = x_prompt, x_sample
    outs = [[] for _ in range(6)]
    c_all = jnp.concatenate([c_prompt, c_sample], axis=0)
    for l in range(depth):
        lam0 = _lambda_init(l)
        mod = _modulation(c_all, w_ada[l].astype(BF16), b_ada[l][None, :]).reshape(B + Bs, 6, D_MODEL)
        w = {
            "n1g": norm1_g[l][None, :], "n2g": norm2_g[l][None, :],
            "w_in": w_in[l].astype(BF16),
            "gq": jnp.tile(q_norm_g[l], D_MODEL // HEAD_DIM)[None, :] * (HEAD_DIM ** -0.5 * LOG2E),
            "gk": jnp.tile(k_norm_g[l], D_MODEL // HEAD_DIM)[None, :],
            "conv_w": conv_w[l],
            "lam_vecs": jnp.stack([lambda_q1[l], lambda_k1[l], lambda_q2[l], lambda_k2[l]]),
            "subg": sub_norm_g[l][None, :],
            "wa": w_attn_out[l].astype(BF16), "wc": w_conv_out[l].astype(BF16), "wo": w_out[l].astype(BF16),
            "wgu": w_gate_up[l].astype(BF16), "wd": w_down[l].astype(BF16),
        }
        zeros_prev = jnp.zeros((B, CONV_WIDTH - 1, D_MODEL), xp.dtype)
        xp, kp, vp, cp = _layer(xp, mod[:B], None, None, zeros_prev, lam0, w, 512)
        xs, ksn, vsn, csn = _layer(xs, mod[B:], cache_k[l].reshape(Bs, past, D_MODEL),
                                   cache_v[l].reshape(Bs, past, D_MODEL), state_conv[l], lam0, w, None)
        for lst, val in zip(outs, (kp, vp, cp, ksn, vsn, csn)):
            lst.append(val)
    return (xp, xs) + tuple(jnp.stack(o) for o in outs)
```

```python
import functools
import math

import jax
import jax.numpy as jnp
from jax import lax
from jax.experimental import pallas as pl
from jax.experimental.pallas import tpu as pltpu

D_MODEL = 1024
N_HEADS = 8
HEAD_DIM = 64
V_DIM = 2 * HEAD_DIM
CONV_WIDTH = 3
CHUNK = 64
D_FF = 2816
EPS = 1e-6
N_IN_GROUPS = 8
LOG2E = math.log2(math.e)
NEG_BIG = -1e30
SCORE_BOUND_MAX = 32.0
ATTN_HEADS_PER_STEP = 2

LANES = 128
MXU_DIM = 256
VMEM_LIMIT = 56 * 1024 * 1024

F32 = jnp.float32
BF16 = jnp.bfloat16


def _lambda_init(layer_idx):
    return 0.8 - 0.6 * math.exp(-0.3 * layer_idx)


def _params(*sem):
    return pltpu.CompilerParams(dimension_semantics=sem, vmem_limit_bytes=VMEM_LIMIT)


def _resident(shape):
    return pl.BlockSpec(shape, lambda *_: (0,) * len(shape), pipeline_mode=pl.Buffered(1))


def _dot(a, b):
    return jnp.dot(a, b, preferred_element_type=F32)


def _dot_nt(a, b):
    return lax.dot_general(a, b, (((1,), (1,)), ((), ())), preferred_element_type=F32)


def _rms_rows(x):
    return x * lax.rsqrt(jnp.mean(x * x, axis=-1, keepdims=True) + EPS)


def _mod_kernel(c_ref, w_ref, b_ref, o_ref):
    c = c_ref[...]
    sc = (c * jax.nn.sigmoid(c)).astype(BF16)
    o_ref[...] = _dot(sc, w_ref[...]) + b_ref[...]


def _modulation(c_all, w_ada, b_ada):
    n = c_all.shape[0]
    return pl.pallas_call(
        _mod_kernel,
        grid=(6,),
        in_specs=[pl.BlockSpec((n, D_MODEL), lambda j: (0, 0)),
                  pl.BlockSpec((D_MODEL, D_MODEL), lambda j: (0, j)),
                  pl.BlockSpec((1, D_MODEL), lambda j: (0, j))],
        out_specs=pl.BlockSpec((n, D_MODEL), lambda j: (0, j)),
        out_shape=jax.ShapeDtypeStruct((n, 6 * D_MODEL), F32),
        compiler_params=_params("arbitrary"),
        name="adaln_mod",
    )(c_all, w_ada, b_ada)


def _group_mean_matrix():
    r = lax.broadcasted_iota(jnp.int32, (MXU_DIM, MXU_DIM), 0) // HEAD_DIM
    c = lax.broadcasted_iota(jnp.int32, (MXU_DIM, MXU_DIM), 1) // HEAD_DIM
    return jnp.where(r == c, 1.0 / HEAD_DIM, 0.0).astype(BF16)


def _group_rms(r, gmat):
    r2 = (r * r).astype(BF16)
    ms = jnp.concatenate(
        [_dot(r2[:, c * MXU_DIM:(c + 1) * MXU_DIM], gmat) for c in range(D_MODEL // MXU_DIM)], axis=1)
    return r * lax.rsqrt(ms + EPS)


def _inproj_kernel(x_ref, mod_ref, prev_ref, n1g_ref, w_ref, gq_ref, gk_ref, cw_ref,
                   qb_ref, kf_ref, kb_ref, vf_ref, vb_ref, yb_ref, sa_ref, sb_ref, tail_ref,
                   carry_ref, *, v_transposed):
    t = pl.program_id(1)
    tm = x_ref.shape[1]

    x = x_ref[0]
    h = (_rms_rows(x) * n1g_ref[...]) * (1.0 + mod_ref[0, 1:2, :]) + mod_ref[0, 0:1, :]
    h = h.astype(BF16)

    def proj(j):
        return _dot(h, w_ref[:, j * D_MODEL:(j + 1) * D_MODEL])

    gmat = _group_mean_matrix()

    rq = proj(0)
    rk = proj(1)
    qb_ref[0] = (_group_rms(rq, gmat) * gq_ref[...]).astype(BF16)

    v = proj(2)
    kn = _group_rms(rk, gmat) * gk_ref[...]
    kf_ref[0] = kn
    kb_ref[0] = kn.astype(BF16)
    vf_ref[0] = v
    vb_ref[0] = v.T.astype(BF16) if v_transposed else v.astype(BF16)

    @pl.when(t == 0)
    def _():
        carry_ref[0:2, :] = prev_ref[0]

    u = proj(5) * proj(3)
    p0 = carry_ref[0:1, :]
    p1 = carry_ref[1:2, :]
    row = lax.broadcasted_iota(jnp.int32, (tm, D_MODEL), 0)
    u1 = jnp.where(row == 0, p1, pltpu.roll(u, 1, 0))
    u2 = jnp.where(row == 0, p0, jnp.where(row == 1, p1, pltpu.roll(u, 2, 0)))
    cv = u2 * cw_ref[0:1, :] + u1 * cw_ref[1:2, :] + u * cw_ref[2:3, :]
    tail = u[tm - 2:tm, :]
    carry_ref[0:2, :] = tail
    tail_ref[0] = tail
    yb_ref[0] = (proj(4) * cv).astype(BF16)

    sa_ref[0] = jax.nn.sigmoid(proj(6)).astype(BF16)
    sb_ref[0] = jax.nn.sigmoid(proj(7)).astype(BF16)


def _in_projection(x, mod, conv_prev, n1g, w_in, gq, gk, conv_w, tm, v_transposed):
    B, T, _ = x.shape
    tile = pl.BlockSpec((1, tm, D_MODEL), lambda b, t: (b, t, 0))
    per_b2 = pl.BlockSpec((1, CONV_WIDTH - 1, D_MODEL), lambda b, t: (b, 0, 0))
    row = _resident((1, D_MODEL))
    act_bf = jax.ShapeDtypeStruct((B, T, D_MODEL), BF16)
    act_f32 = jax.ShapeDtypeStruct((B, T, D_MODEL), F32)
    if v_transposed:
        vb_spec = pl.BlockSpec((1, D_MODEL, tm), lambda b, t: (b, 0, t))
        vb_shape = jax.ShapeDtypeStruct((B, D_MODEL, T), BF16)
    else:
        vb_spec, vb_shape = tile, act_bf
    return pl.pallas_call(
        functools.partial(_inproj_kernel, v_transposed=v_transposed),
        grid=(B, T // tm),
        in_specs=[tile,
                  pl.BlockSpec((1, 6, D_MODEL), lambda b, t: (b, 0, 0)),
                  per_b2,
                  row,
                  _resident((D_MODEL, N_IN_GROUPS * D_MODEL)),
                  row, row,
                  _resident((CONV_WIDTH, D_MODEL))],
        out_specs=[tile] * 4 + [vb_spec] + [tile] * 3 + [per_b2],
        out_shape=[act_bf, act_f32, act_bf, act_f32, vb_shape, act_bf, act_bf, act_bf,
                   jax.ShapeDtypeStruct((B, CONV_WIDTH - 1, D_MODEL), F32)],
        scratch_shapes=[pltpu.VMEM((8, D_MODEL), F32)],
        compiler_params=_params("arbitrary", "arbitrary"),
        name="in_projection",
    )(x, mod, conv_prev, n1g, w_in, gq, gk, conv_w)


def _stack_maps(q):
    lane = lax.broadcasted_iota(jnp.int32, q.shape, 1)
    zero = jnp.zeros_like(q)
    return jnp.concatenate([jnp.where(lane < HEAD_DIM, q, zero), jnp.where(lane >= HEAD_DIM, q, zero)], axis=0)


def _lambda_full(lam_ref, lam0):
    lv = lam_ref[...]
    a = jnp.sum(lv[0:1, :] * lv[1:2, :], axis=-1, keepdims=True)
    b = jnp.sum(lv[2:3, :] * lv[3:4, :], axis=-1, keepdims=True)
    return jnp.exp(a) - jnp.exp(b) + lam0


def _finish_head(o1, o2, lam, subg, lam0):
    o = o1 - lam * o2
    return (_rms_rows(o) * subg) * (1.0 - lam0)


def _attn_prompt_kernel(qa_ref, qb_ref, k_ref, vt_ref, lam_ref, subg_ref, olo_ref, ohi_ref,
                        qs_scr, bias_scr, m_scr, l_scr, acc_scr, s0_scr, s1_scr, *, lam0, running_max):
    i = pl.program_id(2)
    tq = qa_ref.shape[1]
    tk = s0_scr.shape[0]
    ks = tq // tk
    cols = 2 * tq
    n_heads = qa_ref.shape[2] // V_DIM
    n_tiles = k_ref.shape[1] // tq
    n_slots = (n_tiles + 1) * ks
    first_b = (i + 1) * ks
    surely_b = (n_tiles // 2) * ks

    def lanes(hh):
        return slice(hh * V_DIM, (hh + 1) * V_DIM)

    @pl.when((pl.program_id(0) == 0) & (pl.program_id(1) == 0) & (i == 0))
    def _():
        kchunk = lax.broadcasted_iota(jnp.int32, (tq, cols), 0) // CHUNK
        qchunk = (lax.broadcasted_iota(jnp.int32, (tq, cols), 1) % tq) // CHUNK
        bias_scr[...] = jnp.where(kchunk <= qchunk, 0.0, NEG_BIG)

    for hh in range(n_heads):
        qs_scr[2 * hh] = _stack_maps(qa_ref[0, :, lanes(hh)])
        qs_scr[2 * hh + 1] = _stack_maps(qb_ref[0, :, lanes(hh)])
    if running_max:
        m_scr[...] = jnp.full(m_scr.shape, NEG_BIG, F32)
    l_scr[...] = jnp.zeros(l_scr.shape, F32)
    acc_scr[...] = jnp.zeros(acc_scr.shape, F32)
    s_bufs = (s0_scr, s1_scr)

    def slot(hh, t):
        if t >= surely_b:
            return 2 * hh + 1, t - first_b, (1.0 if t >= n_slots - ks else None)
        is_a = t < first_b
        return (2 * hh + jnp.where(is_a, 0, 1), jnp.where(is_a, t, t - first_b),
                jnp.where(is_a & (t >= first_b - ks), 1.0, 0.0).astype(F32))

    def scores(hh, t, s_ref):
        sel, kt, _ = slot(hh, t)
        k = k_ref[0, pl.ds(pl.multiple_of(kt * tk, tk), tk), lanes(hh)]
        s_ref[...] = _dot_nt(k, qs_scr[sel])

    def update(hh, t, s_ref):
        sel, kt, diag = slot(hh, t)
        vt = vt_ref[0, lanes(hh), pl.ds(pl.multiple_of(kt * tk, tk), tk)]
        s = s_ref[...]
        if diag is not None:
            s = s + bias_scr[(t % ks) * tk:(t % ks + 1) * tk, :] * diag
        if running_max:
            m_prev = m_scr[sel]
            m_next = jnp.maximum(m_prev, jnp.max(s, axis=0, keepdims=True))
            alpha = jnp.exp2(m_prev - m_next)
            p = jnp.exp2(s - m_next)
            l_scr[sel] = alpha * l_scr[sel] + jnp.sum(p, axis=0, keepdims=True)
            acc_scr[sel] = alpha * acc_scr[sel] + _dot(vt, p.astype(BF16))
            m_scr[sel] = m_next
        else:
            p = jnp.exp2(s)
            l_scr[sel] += jnp.sum(p, axis=0, keepdims=True)
            acc_scr[sel] += _dot(vt, p.astype(BF16))

    lam = _lambda_full(lam_ref, lam0)

    def finish(hh, sel, o_ref):
        ot = acc_scr[2 * hh + sel] * (1.0 / l_scr[2 * hh + sel])
        o = ot[:, :tq] - lam * ot[:, tq:]
        o = (o * lax.rsqrt(jnp.mean(o * o, axis=0, keepdims=True) + EPS) * subg_ref[...]) * (1.0 - lam0)
        o_ref[0, :, lanes(hh)] = o.T.astype(o_ref.dtype)

    sweep = [(hh, t) for hh in range(n_heads) for t in range(n_slots)]
    scores(*sweep[0], s_bufs[0])
    for g, (hh, t) in enumerate(sweep):
        if g + 1 < len(sweep):
            scores(*sweep[g + 1], s_bufs[(g + 1) % 2])
        update(hh, t, s_bufs[g % 2])
        if t == surely_b - 1:
            finish(hh, 0, olo_ref)
        if t == n_slots - 1:
            finish(hh, 1, ohi_ref)


def _attention_prompt(qb, kb, vt, lam_vecs, subg_col, lam0, tq, tk, running_max):
    assert tq % tk == 0 and tk % CHUNK == 0
    B, S, _ = qb.shape
    n_tiles = S // tq
    half = n_tiles // 2
    hp = ATTN_HEADS_PER_STEP
    tile = (1, tq, hp * V_DIM)
    return pl.pallas_call(
        functools.partial(_attn_prompt_kernel, lam0=lam0, running_max=running_max),
        grid=(B, N_HEADS // hp, half),
        in_specs=[pl.BlockSpec(tile, lambda b, h, i: (b, i, h)),
                  pl.BlockSpec(tile, lambda b, h, i: (b, n_tiles - 1 - i, h)),
                  pl.BlockSpec((1, S, hp * V_DIM), lambda b, h, i: (b, 0, h)),
                  pl.BlockSpec((1, hp * V_DIM, S), lambda b, h, i: (b, h, 0)),
                  pl.BlockSpec((4, HEAD_DIM), lambda b, h, i: (0, 0)),
                  pl.BlockSpec((V_DIM, 1), lambda b, h, i: (0, 0))],
        out_specs=[pl.BlockSpec(tile, lambda b, h, i: (b, i, h)),
                   pl.BlockSpec(tile, lambda b, h, i: (b, half - 1 - i, h))],
        out_shape=[jax.ShapeDtypeStruct((B, S // 2, D_MODEL), BF16)] * 2,
        scratch_shapes=[pltpu.VMEM((2 * hp, 2 * tq, V_DIM), BF16),
                        pltpu.VMEM((tq, 2 * tq), F32),
                        pltpu.VMEM((2 * hp, 1, 2 * tq), F32),
                        pltpu.VMEM((2 * hp, 1, 2 * tq), F32),
                        pltpu.VMEM((2 * hp, V_DIM, 2 * tq), F32),
                        pltpu.VMEM((tk, 2 * tq), F32),
                        pltpu.VMEM((tk, 2 * tq), F32)],
        compiler_params=_params("arbitrary", "arbitrary", "arbitrary"),
        name="attn_prompt_running_max" if running_max else "attn_prompt_direct",
    )(qb, qb, kb, vt, lam_vecs, subg_col)


def _attn_sample_kernel(q_ref, kn_ref, vn_ref, kc_ref, vc_ref, lam_ref, subg_ref, o_ref, *, lam0):
    tq = q_ref.shape[1]
    lam = _lambda_full(lam_ref, lam0)
    for h in range(N_HEADS):
        cols = slice(h * V_DIM, (h + 1) * V_DIM)
        qs = _stack_maps(q_ref[0, :, cols])
        s_c = _dot_nt(qs, kc_ref[0, :, cols])
        s_n = _dot_nt(qs, kn_ref[0, :, cols])
        m = jnp.maximum(jnp.max(s_c, axis=1, keepdims=True), jnp.max(s_n, axis=1, keepdims=True))
        p_c = jnp.exp2(s_c - m)
        p_n = jnp.exp2(s_n - m)
        l = jnp.sum(p_c, axis=1, keepdims=True) + jnp.sum(p_n, axis=1, keepdims=True)
        acc = _dot(p_c.astype(BF16), vc_ref[0, :, cols]) + _dot(p_n.astype(BF16), vn_ref[0, :, cols])
        o = acc / l
        o_ref[0, :, cols] = _finish_head(o[:tq], o[tq:], lam, subg_ref[...], lam0).astype(o_ref.dtype)


def _attention_sample(qb, kb, vb, cache_k, cache_v, lam_vecs, subg, lam0):
    B, T, _ = qb.shape
    P = cache_k.shape[1]
    new_spec = pl.BlockSpec((1, T, D_MODEL), lambda b: (b, 0, 0))
    past_spec = pl.BlockSpec((1, P, D_MODEL), lambda b: (b, 0, 0))
    return pl.pallas_call(
        functools.partial(_attn_sample_kernel, lam0=lam0),
        grid=(B,),
        in_specs=[new_spec, new_spec, new_spec, past_spec, past_spec,
                  pl.BlockSpec((4, HEAD_DIM), lambda b: (0, 0)),
                  pl.BlockSpec((1, V_DIM), lambda b: (0, 0))],
        out_specs=new_spec,
        out_shape=jax.ShapeDtypeStruct((B, T, D_MODEL), BF16),
        compiler_params=_params("arbitrary"),
        name="attn_sample",
    )(qb, kb, vb, cache_k, cache_v, lam_vecs, subg)


def _merge_kernel(x_ref, mod_ref, olo_ref, ohi_ref, yb_ref, sa_ref, sb_ref, wa_ref, wc_ref, wo_ref, x1_ref,
                  *, n_lo):
    o = jnp.where(pl.program_id(1) < n_lo, olo_ref[0], ohi_ref[0])
    y_a = _dot(o, wa_ref[...])
    y_b = _dot(yb_ref[0], wc_ref[...])
    m = sa_ref[0].astype(F32) * y_a + sb_ref[0].astype(F32) * y_b
    x1_ref[0] = x_ref[0] + mod_ref[0, 2:3, :] * _dot(m.astype(BF16), wo_ref[...])


def _merge(x, mod, o_lo, o_hi, yb, sa, sb, wa, wc, wo, tm):
    B, T, _ = x.shape
    n_lo = o_lo.shape[1] // tm
    tile = pl.BlockSpec((1, tm, D_MODEL), lambda b, t: (b, t, 0))
    wspec = _resident((D_MODEL, D_MODEL))
    return pl.pallas_call(
        functools.partial(_merge_kernel, n_lo=n_lo),
        grid=(B, T // tm),
        in_specs=[tile, pl.BlockSpec((1, 6, D_MODEL), lambda b, t: (b, 0, 0)),
                  pl.BlockSpec((1, tm, D_MODEL), lambda b, t: (b, jnp.minimum(t, n_lo - 1), 0)),
                  pl.BlockSpec((1, tm, D_MODEL), lambda b, t: (b, jnp.maximum(t - n_lo, 0), 0)),
                  tile, tile, tile, wspec, wspec, wspec],
        out_specs=tile,
        out_shape=jax.ShapeDtypeStruct((B, T, D_MODEL), F32),
        compiler_params=_params("arbitrary", "arbitrary"),
        name="branch_merge",
    )(x, mod, o_lo, o_hi, yb, sa, sb, wa, wc, wo)


def _ffn_chunks():
    step = 2 * MXU_DIM
    return [(c0, min(step, D_FF - c0)) for c0 in range(0, D_FF, step)]


def _ffn_kernel(x_ref, mod_ref, n2g_ref, wgu_ref, wd_ref, y_ref):
    x1 = x_ref[0]
    h2 = (_rms_rows(x1) * n2g_ref[...]) * (1.0 + mod_ref[0, 4:5, :]) + mod_ref[0, 3:4, :]
    h2 = h2.astype(BF16)
    acc = jnp.zeros(x1.shape, F32)
    for c0, cs in _ffn_chunks():
        g = _dot(h2, wgu_ref[:, c0:c0 + cs])
        up = _dot(h2, wgu_ref[:, D_FF + c0:D_FF + c0 + cs])
        a = ((g * jax.nn.sigmoid(g)) * up).astype(BF16)
        acc = acc + _dot(a, wd_ref[c0:c0 + cs, :])
    y_ref[0] = x1 + mod_ref[0, 5:6, :] * acc


def _ffn(x1, mod, n2g, wgu, wd, tm):
    B, T, _ = x1.shape
    tile = pl.BlockSpec((1, tm, D_MODEL), lambda b, t: (b, t, 0))
    return pl.pallas_call(
        _ffn_kernel,
        grid=(B, T // tm),
        in_specs=[tile, pl.BlockSpec((1, 6, D_MODEL), lambda b, t: (b, 0, 0)),
                  _resident((1, D_MODEL)),
                  _resident((D_MODEL, 2 * D_FF)),
                  _resident((D_FF, D_MODEL))],
        out_specs=tile,
        out_shape=jax.ShapeDtypeStruct((B, T, D_MODEL), F32),
        compiler_params=_params("arbitrary", "arbitrary"),
        name="swiglu_ffn",
    )(x1, mod, n2g, wgu, wd)


def _tile_rows(T):
    return min(T, 512)


def _layer(x, mod, past_k, past_v, conv_prev, lam0, w, attn_tiles):
    B, T, _ = x.shape
    tm = _tile_rows(T)
    qb, kf, kb, vf, vb, yb, sa, sb, tail = _in_projection(
        x, mod, conv_prev, w["n1g"], w["w_in"], w["gq"], w["gk"], w["conv_w"], tm, past_k is None)
    if past_k is None:
        score_bound = HEAD_DIM * jnp.max(jnp.abs(w["gq"])) * jnp.max(jnp.abs(w["gk"]))
        attend = functools.partial(_attention_prompt, qb, kb, vb, w["lam_vecs"], w["subg"].reshape(V_DIM, 1),
                                   lam0, *attn_tiles)
        o_lo, o_hi = lax.cond(score_bound <= SCORE_BOUND_MAX, lambda: attend(False), lambda: attend(True))
    else:
        o_lo = o_hi = _attention_sample(qb, kb, vb, past_k, past_v, w["lam_vecs"], w["subg"], lam0)
    x1 = _merge(x, mod, o_lo, o_hi, yb, sa, sb, w["wa"], w["wc"], w["wo"], tm)
    y = _ffn(x1, mod, w["n2g"], w["wgu"], w["wd"], tm)
    return (y, kf.reshape(B, T, N_HEADS, 2, HEAD_DIM), vf.reshape(B, T, N_HEADS, V_DIM), tail)


def kernel(x_prompt, x_sample, cache_k, cache_v, state_conv, c_prompt, c_sample, w_ada, b_ada, norm1_g, norm2_g, w_in, q_norm_g, k_norm_g, lambda_q1, lambda_k1, lambda_q2, lambda_k2, sub_norm_g, w_attn_out, conv_w, w_conv_out, w_out, w_gate_up, w_down):
    depth = w_ada.shape[0]
    B = x_prompt.shape[0]
    Bs, _, _ = x_sample.shape
    past = cache_k.shape[2]
    xp, xs = x_prompt, x_sample
    outs = [[] for _ in range(6)]
    c_all = jnp.concatenate([c_prompt, c_sample], axis=0)
    for l in range(depth):
        lam0 = _lambda_init(l)
        mod = _modulation(c_all, w_ada[l].astype(BF16), b_ada[l][None, :]).reshape(B + Bs, 6, D_MODEL)
        w = {
            "n1g": norm1_g[l][None, :], "n2g": norm2_g[l][None, :],
            "w_in": w_in[l].astype(BF16),
            "gq": jnp.tile(q_norm_g[l], D_MODEL // HEAD_DIM)[None, :] * (HEAD_DIM ** -0.5 * LOG2E),
            "gk": jnp.tile(k_norm_g[l], D_MODEL // HEAD_DIM)[None, :],
            "conv_w": conv_w[l],
            "lam_vecs": jnp.stack([lambda_q1[l], lambda_k1[l], lambda_q2[l], lambda_k2[l]]),
            "subg": sub_norm_g[l][None, :],
            "wa": w_attn_out[l].astype(BF16), "wc": w_conv_out[l].astype(BF16), "wo": w_out[l].astype(BF16),
            "wgu": w_gate_up[l].astype(BF16), "wd": w_down[l].astype(BF16),
        }
        zeros_prev = jnp.zeros((B, CONV_WIDTH - 1, D_MODEL), xp.dtype)
        xp, kp, vp, cp = _layer(xp, mod[:B], None, None, zeros_prev, lam0, w, (512, 256))
        xs, ksn, vsn, csn = _layer(xs, mod[B:], cache_k[l].astype(BF16).reshape(Bs, past, D_MODEL),
                                   cache_v[l].astype(BF16).reshape(Bs, past, D_MODEL), state_conv[l], lam0, w, None)
        for lst, val in zip(outs, (kp, vp, cp, ksn, vsn, csn)):
            lst.append(val)
    return (xp, xs) + tuple(jnp.stack(o) for o in outs)
```

```python
import functools
import math

import jax
import jax.numpy as jnp
from jax import lax
from jax.experimental import pallas as pl
from jax.experimental.pallas import tpu as pltpu

D_MODEL = 1024
N_HEADS = 8
HEAD_DIM = 64
V_DIM = 2 * HEAD_DIM
CONV_WIDTH = 3
CHUNK = 64
D_FF = 2816
EPS = 1e-6
N_IN_GROUPS = 8
LOG2E = math.log2(math.e)
NEG_BIG = -1e30
SCORE_BOUND_MAX = 32.0
ATTN_HEADS_PER_STEP = 2

LANES = 128
MXU_DIM = 256
VMEM_LIMIT = 56 * 1024 * 1024

F32 = jnp.float32
BF16 = jnp.bfloat16


def _lambda_init(layer_idx):
    return 0.8 - 0.6 * math.exp(-0.3 * layer_idx)


def _params(*sem):
    return pltpu.CompilerParams(dimension_semantics=sem, vmem_limit_bytes=VMEM_LIMIT)


def _resident(shape):
    return pl.BlockSpec(shape, lambda *_: (0,) * len(shape), pipeline_mode=pl.Buffered(1))


def _dot(a, b):
    return jnp.dot(a, b, preferred_element_type=F32)


def _dot_nt(a, b):
    return lax.dot_general(a, b, (((1,), (1,)), ((), ())), preferred_element_type=F32)


def _rms_rows(x):
    return x * lax.rsqrt(jnp.mean(x * x, axis=-1, keepdims=True) + EPS)


def _mod_kernel(c_ref, w_ref, b_ref, o_ref):
    c = c_ref[...]
    sc = (c * jax.nn.sigmoid(c)).astype(BF16)
    o_ref[...] = _dot(sc, w_ref[...]) + b_ref[...]


def _modulation(c_all, w_ada, b_ada):
    n = c_all.shape[0]
    return pl.pallas_call(
        _mod_kernel,
        grid=(6,),
        in_specs=[pl.BlockSpec((n, D_MODEL), lambda j: (0, 0)),
                  pl.BlockSpec((D_MODEL, D_MODEL), lambda j: (0, j)),
                  pl.BlockSpec((1, D_MODEL), lambda j: (0, j))],
        out_specs=pl.BlockSpec((n, D_MODEL), lambda j: (0, j)),
        out_shape=jax.ShapeDtypeStruct((n, 6 * D_MODEL), F32),
        compiler_params=_params("arbitrary"),
        name="adaln_mod",
    )(c_all, w_ada, b_ada)


def _group_mean_matrix():
    r = lax.broadcasted_iota(jnp.int32, (MXU_DIM, MXU_DIM), 0) // HEAD_DIM
    c = lax.broadcasted_iota(jnp.int32, (MXU_DIM, MXU_DIM), 1) // HEAD_DIM
    return jnp.where(r == c, 1.0 / HEAD_DIM, 0.0).astype(BF16)


def _group_rms(r, gmat):
    r2 = (r * r).astype(BF16)
    ms = jnp.concatenate(
        [_dot(r2[:, c * MXU_DIM:(c + 1) * MXU_DIM], gmat) for c in range(D_MODEL // MXU_DIM)], axis=1)
    return r * lax.rsqrt(ms + EPS)


def _inproj_kernel(x_ref, mod_ref, prev_ref, n1g_ref, w_ref, gq_ref, gk_ref, cw_ref,
                   qb_ref, kf_ref, kb_ref, vf_ref, vb_ref, yb_ref, sa_ref, sb_ref, tail_ref,
                   carry_ref, *, v_transposed):
    t = pl.program_id(1)
    tm = x_ref.shape[1]

    x = x_ref[0]
    h = (_rms_rows(x) * n1g_ref[...]) * (1.0 + mod_ref[0, 1:2, :]) + mod_ref[0, 0:1, :]
    h = h.astype(BF16)

    def proj(j):
        return _dot(h, w_ref[:, j * D_MODEL:(j + 1) * D_MODEL])

    gmat = _group_mean_matrix()

    rq = proj(0)
    rk = proj(1)
    qb_ref[0] = (_group_rms(rq, gmat) * gq_ref[...]).astype(BF16)

    v = proj(2)
    kn = _group_rms(rk, gmat) * gk_ref[...]
    kf_ref[0] = kn
    kb_ref[0] = kn.astype(BF16)
    vf_ref[0] = v
    if not v_transposed:
        vb_ref[0] = v.astype(BF16)

    @pl.when(t == 0)
    def _():
        carry_ref[0:2, :] = prev_ref[0]

    u = proj(5) * proj(3)
    p0 = carry_ref[0:1, :]
    p1 = carry_ref[1:2, :]
    row = lax.broadcasted_iota(jnp.int32, (tm, D_MODEL), 0)
    u1 = jnp.where(row == 0, p1, pltpu.roll(u, 1, 0))
    u2 = jnp.where(row == 0, p0, jnp.where(row == 1, p1, pltpu.roll(u, 2, 0)))
    cv = u2 * cw_ref[0:1, :] + u1 * cw_ref[1:2, :] + u * cw_ref[2:3, :]
    tail = u[tm - 2:tm, :]
    carry_ref[0:2, :] = tail
    tail_ref[0] = tail
    yb_ref[0] = (proj(4) * cv).astype(BF16)

    sa_ref[0] = jax.nn.sigmoid(proj(6)).astype(BF16)
    sb_ref[0] = jax.nn.sigmoid(proj(7)).astype(BF16)
    if v_transposed:
        vb_ref[0] = vf_ref[0].T.astype(BF16)


def _in_projection(x, mod, conv_prev, n1g, w_in, gq, gk, conv_w, tm, v_transposed):
    B, T, _ = x.shape
    tile = pl.BlockSpec((1, tm, D_MODEL), lambda b, t: (b, t, 0))
    per_b2 = pl.BlockSpec((1, CONV_WIDTH - 1, D_MODEL), lambda b, t: (b, 0, 0))
    row = _resident((1, D_MODEL))
    act_bf = jax.ShapeDtypeStruct((B, T, D_MODEL), BF16)
    act_f32 = jax.ShapeDtypeStruct((B, T, D_MODEL), F32)
    if v_transposed:
        vb_spec = pl.BlockSpec((1, D_MODEL, tm), lambda b, t: (b, 0, t))
        vb_shape = jax.ShapeDtypeStruct((B, D_MODEL, T), BF16)
    else:
        vb_spec, vb_shape = tile, act_bf
    return pl.pallas_call(
        functools.partial(_inproj_kernel, v_transposed=v_transposed),
        grid=(B, T // tm),
        in_specs=[tile,
                  pl.BlockSpec((1, 6, D_MODEL), lambda b, t: (b, 0, 0)),
                  per_b2,
                  row,
                  _resident((D_MODEL, N_IN_GROUPS * D_MODEL)),
                  row, row,
                  _resident((CONV_WIDTH, D_MODEL))],
        out_specs=[tile] * 4 + [vb_spec] + [tile] * 3 + [per_b2],
        out_shape=[act_bf, act_f32, act_bf, act_f32, vb_shape, act_bf, act_bf, act_bf,
                   jax.ShapeDtypeStruct((B, CONV_WIDTH - 1, D_MODEL), F32)],
        scratch_shapes=[pltpu.VMEM((8, D_MODEL), F32)],
        compiler_params=_params("arbitrary", "arbitrary"),
        name="in_projection",
    )(x, mod, conv_prev, n1g, w_in, gq, gk, conv_w)


def _stack_maps(q):
    lane = lax.broadcasted_iota(jnp.int32, q.shape, 1)
    zero = jnp.zeros_like(q)
    return jnp.concatenate([jnp.where(lane < HEAD_DIM, q, zero), jnp.where(lane >= HEAD_DIM, q, zero)], axis=0)


def _lambda_full(lam_ref, lam0):
    lv = lam_ref[...]
    a = jnp.sum(lv[0:1, :] * lv[1:2, :], axis=-1, keepdims=True)
    b = jnp.sum(lv[2:3, :] * lv[3:4, :], axis=-1, keepdims=True)
    return jnp.exp(a) - jnp.exp(b) + lam0


def _finish_head(o1, o2, lam, subg, lam0):
    o = o1 - lam * o2
    return (_rms_rows(o) * subg) * (1.0 - lam0)


def _attn_prompt_kernel(qa_ref, qb_ref, k_ref, vt_ref, lam_ref, subg_ref, olo_ref, ohi_ref,
                        qs_scr, bias_scr, m_scr, l_scr, acc_scr, s0_scr, s1_scr, *, lam0, running_max):
    i = pl.program_id(2)
    tq = qa_ref.shape[1]
    tk = s0_scr.shape[0]
    ks = tq // tk
    cols = 2 * tq
    n_heads = qa_ref.shape[2] // V_DIM
    n_tiles = k_ref.shape[1] // tq
    n_slots = (n_tiles + 1) * ks
    first_b = (i + 1) * ks
    surely_b = (n_tiles // 2) * ks

    def lanes(hh):
        return slice(hh * V_DIM, (hh + 1) * V_DIM)

    @pl.when((pl.program_id(0) == 0) & (pl.program_id(1) == 0) & (i == 0))
    def _():
        kchunk = lax.broadcasted_iota(jnp.int32, (tq, cols), 0) // CHUNK
        qchunk = (lax.broadcasted_iota(jnp.int32, (tq, cols), 1) % tq) // CHUNK
        bias_scr[...] = jnp.where(kchunk <= qchunk, 0.0, NEG_BIG)

    for hh in range(n_heads):
        qs_scr[2 * hh] = _stack_maps(qa_ref[0, :, lanes(hh)])
        qs_scr[2 * hh + 1] = _stack_maps(qb_ref[0, :, lanes(hh)])
    if running_max:
        m_scr[...] = jnp.full(m_scr.shape, NEG_BIG, F32)
    l_scr[...] = jnp.zeros(l_scr.shape, F32)
    acc_scr[...] = jnp.zeros(acc_scr.shape, F32)
    s_bufs = (s0_scr, s1_scr)

    def slot(hh, t):
        if t < ks:
            return 2 * hh, i * ks + t, t
        if t >= surely_b:
            return 2 * hh + 1, t - first_b, (t - (n_slots - ks) if t >= n_slots - ks else None)
        is_a = t < first_b
        return 2 * hh + jnp.where(is_a, 0, 1), jnp.where(is_a, t - ks, t - first_b), None

    def windows(d):
        if d is None or d == 0:
            return [(0, cols)]
        return [(d * tk, tq), (tq + d * tk, cols)]

    def gather(ref, sel, wins, axis):
        if len(wins) == 1:
            return ref[sel]
        parts = [ref[sel, lo:hi, :] if axis == 0 else ref[sel, :, lo:hi] for lo, hi in wins]
        return jnp.concatenate(parts, axis=axis)

    def scatter(ref, sel, wins, val):
        if len(wins) == 1:
            ref[sel] = val
            return
        off = 0
        for lo, hi in wins:
            ref[sel, :, lo:hi] = val[:, off:off + hi - lo]
            off += hi - lo

    def scores(hh, t, s_ref):
        sel, kt, d = slot(hh, t)
        wins = windows(d)
        width = sum(hi - lo for lo, hi in wins)
        k = k_ref[0, pl.ds(pl.multiple_of(kt * tk, tk), tk), lanes(hh)]
        s_ref[:, :width] = _dot_nt(k, gather(qs_scr, sel, wins, 0))

    def update(hh, t, s_ref):
        sel, kt, d = slot(hh, t)
        wins = windows(d)
        width = sum(hi - lo for lo, hi in wins)
        vt = vt_ref[0, lanes(hh), pl.ds(pl.multiple_of(kt * tk, tk), tk)]
        s = s_ref[:, :width]
        if d is not None:
            s = s + jnp.concatenate([bias_scr[d * tk:(d + 1) * tk, lo:hi] for lo, hi in wins], axis=1)
        if running_max:
            m_prev = gather(m_scr, sel, wins, 1)
            m_next = jnp.maximum(m_prev, jnp.max(s, axis=0, keepdims=True))
            alpha = jnp.exp2(m_prev - m_next)
            p = jnp.exp2(s - m_next)
            scatter(l_scr, sel, wins, alpha * gather(l_scr, sel, wins, 1) + jnp.sum(p, axis=0, keepdims=True))
            scatter(acc_scr, sel, wins, alpha * gather(acc_scr, sel, wins, 1) + _dot(vt, p.astype(BF16)))
            scatter(m_scr, sel, wins, m_next)
        else:
            p = jnp.exp2(s)
            scatter(l_scr, sel, wins, gather(l_scr, sel, wins, 1) + jnp.sum(p, axis=0, keepdims=True))
            scatter(acc_scr, sel, wins, gather(acc_scr, sel, wins, 1) + _dot(vt, p.astype(BF16)))

    lam = _lambda_full(lam_ref, lam0)

    def finish(hh, sel, o_ref):
        ot = acc_scr[2 * hh + sel] * (1.0 / l_scr[2 * hh + sel])
        o = ot[:, :tq] - lam * ot[:, tq:]
        o = (o * lax.rsqrt(jnp.mean(o * o, axis=0, keepdims=True) + EPS) * subg_ref[...]) * (1.0 - lam0)
        o_ref[0, :, lanes(hh)] = o.T.astype(o_ref.dtype)

    sweep = [(hh, t) for hh in range(n_heads) for t in range(n_slots)]
    scores(*sweep[0], s_bufs[0])
    for g, (hh, t) in enumerate(sweep):
        if g + 1 < len(sweep):
            scores(*sweep[g + 1], s_bufs[(g + 1) % 2])
        update(hh, t, s_bufs[g % 2])
        if t == surely_b - 1:
            finish(hh, 0, olo_ref)
        if t == n_slots - 1:
            finish(hh, 1, ohi_ref)


def _attention_prompt(qb, kb, vt, lam_vecs, subg_col, lam0, tq, tk, running_max):
    assert tq % tk == 0 and tk % CHUNK == 0
    B, S, _ = qb.shape
    n_tiles = S // tq
    half = n_tiles // 2
    hp = ATTN_HEADS_PER_STEP
    tile = (1, tq, hp * V_DIM)
    return pl.pallas_call(
        functools.partial(_attn_prompt_kernel, lam0=lam0, running_max=running_max),
        grid=(B, N_HEADS // hp, half),
        in_specs=[pl.BlockSpec(tile, lambda b, h, i: (b, i, h)),
                  pl.BlockSpec(tile, lambda b, h, i: (b, n_tiles - 1 - i, h)),
                  pl.BlockSpec((1, S, hp * V_DIM), lambda b, h, i: (b, 0, h)),
                  pl.BlockSpec((1, hp * V_DIM, S), lambda b, h, i: (b, h, 0)),
                  pl.BlockSpec((4, HEAD_DIM), lambda b, h, i: (0, 0)),
                  pl.BlockSpec((V_DIM, 1), lambda b, h, i: (0, 0))],
        out_specs=[pl.BlockSpec(tile, lambda b, h, i: (b, i, h)),
                   pl.BlockSpec(tile, lambda b, h, i: (b, half - 1 - i, h))],
        out_shape=[jax.ShapeDtypeStruct((B, S // 2, D_MODEL), BF16)] * 2,
        scratch_shapes=[pltpu.VMEM((2 * hp, 2 * tq, V_DIM), BF16),
                        pltpu.VMEM((tq, 2 * tq), F32),
                        pltpu.VMEM((2 * hp, 1, 2 * tq), F32),
                        pltpu.VMEM((2 * hp, 1, 2 * tq), F32),
                        pltpu.VMEM((2 * hp, V_DIM, 2 * tq), F32),
                        pltpu.VMEM((tk, 2 * tq), F32),
                        pltpu.VMEM((tk, 2 * tq), F32)],
        compiler_params=_params("arbitrary", "arbitrary", "arbitrary"),
        name="attn_prompt_running_max" if running_max else "attn_prompt_direct",
    )(qb, qb, kb, vt, lam_vecs, subg_col)


def _attn_sample_kernel(q_ref, kn_ref, vn_ref, kc_ref, vc_ref, lam_ref, subg_ref, o_ref, *, lam0):
    tq = q_ref.shape[1]
    lam = _lambda_full(lam_ref, lam0)
    for h in range(N_HEADS):
        cols = slice(h * V_DIM, (h + 1) * V_DIM)
        qs = _stack_maps(q_ref[0, :, cols])
        s_c = _dot_nt(qs, kc_ref[0, :, cols].astype(BF16))
        s_n = _dot_nt(qs, kn_ref[0, :, cols])
        m = jnp.maximum(jnp.max(s_c, axis=1, keepdims=True), jnp.max(s_n, axis=1, keepdims=True))
        p_c = jnp.exp2(s_c - m)
        p_n = jnp.exp2(s_n - m)
        l = jnp.sum(p_c, axis=1, keepdims=True) + jnp.sum(p_n, axis=1, keepdims=True)
        acc = (_dot(p_c.astype(BF16), vc_ref[0, :, cols].astype(BF16))
               + _dot(p_n.astype(BF16), vn_ref[0, :, cols]))
        o = acc / l
        o_ref[0, :, cols] = _finish_head(o[:tq], o[tq:], lam, subg_ref[...], lam0).astype(o_ref.dtype)


def _attention_sample(qb, kb, vb, cache_k, cache_v, lam_vecs, subg, lam0):
    B, T, _ = qb.shape
    P = cache_k.shape[1]
    new_spec = pl.BlockSpec((1, T, D_MODEL), lambda b: (b, 0, 0))
    past_spec = pl.BlockSpec((1, P, D_MODEL), lambda b: (b, 0, 0))
    return pl.pallas_call(
        functools.partial(_attn_sample_kernel, lam0=lam0),
        grid=(B,),
        in_specs=[new_spec, new_spec, new_spec, past_spec, past_spec,
                  pl.BlockSpec((4, HEAD_DIM), lambda b: (0, 0)),
                  pl.BlockSpec((1, V_DIM), lambda b: (0, 0))],
        out_specs=new_spec,
        out_shape=jax.ShapeDtypeStruct((B, T, D_MODEL), BF16),
        compiler_params=_params("arbitrary"),
        name="attn_sample",
    )(qb, kb, vb, cache_k, cache_v, lam_vecs, subg)


def _merge_kernel(x_ref, mod_ref, olo_ref, ohi_ref, yb_ref, sa_ref, sb_ref, wa_ref, wc_ref, wo_ref, x1_ref,
                  *, n_lo):
    o = jnp.where(pl.program_id(1) < n_lo, olo_ref[0], ohi_ref[0])
    y_a = _dot(o, wa_ref[...])
    y_b = _dot(yb_ref[0], wc_ref[...])
    m = sa_ref[0].astype(F32) * y_a + sb_ref[0].astype(F32) * y_b
    x1_ref[0] = x_ref[0] + mod_ref[0, 2:3, :] * _dot(m.astype(BF16), wo_ref[...])


def _merge(x, mod, o_lo, o_hi, yb, sa, sb, wa, wc, wo, tm):
    B, T, _ = x.shape
    n_lo = o_lo.shape[1] // tm
    tile = pl.BlockSpec((1, tm, D_MODEL), lambda b, t: (b, t, 0))
    wspec = _resident((D_MODEL, D_MODEL))
    return pl.pallas_call(
        functools.partial(_merge_kernel, n_lo=n_lo),
        grid=(B, T // tm),
        in_specs=[tile, pl.BlockSpec((1, 6, D_MODEL), lambda b, t: (b, 0, 0)),
                  pl.BlockSpec((1, tm, D_MODEL), lambda b, t: (b, jnp.minimum(t, n_lo - 1), 0)),
                  pl.BlockSpec((1, tm, D_MODEL), lambda b, t: (b, jnp.maximum(t - n_lo, 0), 0)),
                  tile, tile, tile, wspec, wspec, wspec],
        out_specs=tile,
        out_shape=jax.ShapeDtypeStruct((B, T, D_MODEL), F32),
        compiler_params=_params("arbitrary", "arbitrary"),
        name="branch_merge",
    )(x, mod, o_lo, o_hi, yb, sa, sb, wa, wc, wo)


def _ffn_chunks():
    step = 2 * MXU_DIM
    return [(c0, min(step, D_FF - c0)) for c0 in range(0, D_FF, step)]


def _ffn_kernel(x_ref, mod_ref, n2g_ref, wgu_ref, wd_ref, y_ref):
    x1 = x_ref[0]
    h2 = (_rms_rows(x1) * n2g_ref[...]) * (1.0 + mod_ref[0, 4:5, :]) + mod_ref[0, 3:4, :]
    h2 = h2.astype(BF16)
    acc = jnp.zeros(x1.shape, F32)
    for c0, cs in _ffn_chunks():
        g = _dot(h2, wgu_ref[:, c0:c0 + cs])
        up = _dot(h2, wgu_ref[:, D_FF + c0:D_FF + c0 + cs])
        a = ((g * jax.nn.sigmoid(g)) * up).astype(BF16)
        acc = acc + _dot(a, wd_ref[c0:c0 + cs, :])
    y_ref[0] = x1 + mod_ref[0, 5:6, :] * acc


def _ffn(x1, mod, n2g, wgu, wd, tm):
    B, T, _ = x1.shape
    tile = pl.BlockSpec((1, tm, D_MODEL), lambda b, t: (b, t, 0))
    return pl.pallas_call(
        _ffn_kernel,
        grid=(B, T // tm),
        in_specs=[tile, pl.BlockSpec((1, 6, D_MODEL), lambda b, t: (b, 0, 0)),
                  _resident((1, D_MODEL)),
                  _resident((D_MODEL, 2 * D_FF)),
                  _resident((D_FF, D_MODEL))],
        out_specs=tile,
        out_shape=jax.ShapeDtypeStruct((B, T, D_MODEL), F32),
        compiler_params=_params("arbitrary", "arbitrary"),
        name="swiglu_ffn",
    )(x1, mod, n2g, wgu, wd)


def _tile_rows(T):
    return min(T, 512)


def _layer(x, mod, past_k, past_v, conv_prev, lam0, w, attn_tiles):
    B, T, _ = x.shape
    tm = _tile_rows(T)
    qb, kf, kb, vf, vb, yb, sa, sb, tail = _in_projection(
        x, mod, conv_prev, w["n1g"], w["w_in"], w["gq"], w["gk"], w["conv_w"], tm, past_k is None)
    if past_k is None:
        score_bound = HEAD_DIM * jnp.max(jnp.abs(w["gq"])) * jnp.max(jnp.abs(w["gk"]))
        attend = functools.partial(_attention_prompt, qb, kb, vb, w["lam_vecs"], w["subg"].reshape(V_DIM, 1),
                                   lam0, *attn_tiles)
        o_lo, o_hi = lax.cond(score_bound <= SCORE_BOUND_MAX, lambda: attend(False), lambda: attend(True))
    else:
        o_lo = o_hi = _attention_sample(qb, kb, vb, past_k, past_v, w["lam_vecs"], w["subg"], lam0)
    x1 = _merge(x, mod, o_lo, o_hi, yb, sa, sb, w["wa"], w["wc"], w["wo"], tm)
    y = _ffn(x1, mod, w["n2g"], w["wgu"], w["wd"], tm)
    return (y, kf.reshape(B, T, N_HEADS, 2, HEAD_DIM), vf.reshape(B, T, N_HEADS, V_DIM), tail)


def kernel(x_prompt, x_sample, cache_k, cache_v, state_conv, c_prompt, c_sample, w_ada, b_ada, norm1_g, norm2_g, w_in, q_norm_g, k_norm_g, lambda_q1, lambda_k1, lambda_q2, lambda_k2, sub_norm_g, w_attn_out, conv_w, w_conv_out, w_out, w_gate_up, w_down):
    depth = w_ada.shape[0]
    B = x_prompt.shape[0]
    Bs, _, _ = x_sample.shape
    past = cache_k.shape[2]
    xp, xs = x_prompt, x_sample
    outs = [[] for _ in range(6)]
    c_all = jnp.concatenate([c_prompt, c_sample], axis=0)
    for l in range(depth):
        lam0 = _lambda_init(l)
        mod = _modulation(c_all, w_ada[l].astype(BF16), b_ada[l][None, :]).reshape(B + Bs, 6, D_MODEL)
        w = {
            "n1g": norm1_g[l][None, :], "n2g": norm2_g[l][None, :],
            "w_in": w_in[l].astype(BF16),
            "gq": jnp.tile(q_norm_g[l], D_MODEL // HEAD_DIM)[None, :] * (HEAD_DIM ** -0.5 * LOG2E),
            "gk": jnp.tile(k_norm_g[l], D_MODEL // HEAD_DIM)[None, :],
            "conv_w": conv_w[l],
            "lam_vecs": jnp.stack([lambda_q1[l], lambda_k1[l], lambda_q2[l], lambda_k2[l]]),
            "subg": sub_norm_g[l][None, :],
            "wa": w_attn_out[l].astype(BF16), "wc": w_conv_out[l].astype(BF16), "wo": w_out[l].astype(BF16),
            "wgu": w_gate_up[l].astype(BF16), "wd": w_down[l].astype(BF16),
        }
        zeros_prev = jnp.zeros((B, CONV_WIDTH - 1, D_MODEL), xp.dtype)
        xp, kp, vp, cp = _layer(xp, mod[:B], None, None, zeros_prev, lam0, w, (512, 256))
        xs, ksn, vsn, csn = _layer(xs, mod[B:], cache_k[l].reshape(Bs, past, D_MODEL),
                                   cache_v[l].reshape(Bs, past, D_MODEL), state_conv[l], lam0, w, None)
        for lst, val in zip(outs, (kp, vp, cp, ksn, vsn, csn)):
            lst.append(val)
    return (xp, xs) + tuple(jnp.stack(o) for o in outs)
```

```python
import functools
import math

import jax
import jax.numpy as jnp
from jax import lax
from jax.experimental import pallas as pl
from jax.experimental.pallas import tpu as pltpu

D_MODEL = 1024
N_HEADS = 8
HEAD_DIM = 64
V_DIM = 2 * HEAD_DIM
CONV_WIDTH = 3
CHUNK = 64
D_FF = 2816
EPS = 1e-6
N_IN_GROUPS = 8
LOG2E = math.log2(math.e)
NEG_BIG = -1e30
SCORE_BOUND_MAX = 32.0
ATTN_HEADS_PER_STEP = 4

LANES = 128
MXU_DIM = 256
VMEM_LIMIT = 56 * 1024 * 1024

F32 = jnp.float32
BF16 = jnp.bfloat16


def _lambda_init(layer_idx):
    return 0.8 - 0.6 * math.exp(-0.3 * layer_idx)


def _params(*sem):
    return pltpu.CompilerParams(dimension_semantics=sem, vmem_limit_bytes=VMEM_LIMIT)


def _resident(shape):
    return pl.BlockSpec(shape, lambda *_: (0,) * len(shape), pipeline_mode=pl.Buffered(1))


def _dot(a, b):
    return jnp.dot(a, b, preferred_element_type=F32)


def _dot_nt(a, b):
    return lax.dot_general(a, b, (((1,), (1,)), ((), ())), preferred_element_type=F32)


def _rms_rows(x):
    return x * lax.rsqrt(jnp.mean(x * x, axis=-1, keepdims=True) + EPS)


def _mod_kernel(c_ref, w_ref, b_ref, o_ref):
    c = c_ref[...]
    sc = (c * jax.nn.sigmoid(c)).astype(BF16)
    o_ref[...] = _dot(sc, w_ref[...]) + b_ref[...]


def _modulation(c_all, w_ada, b_ada):
    n = c_all.shape[0]
    return pl.pallas_call(
        _mod_kernel,
        grid=(6,),
        in_specs=[pl.BlockSpec((n, D_MODEL), lambda j: (0, 0)),
                  pl.BlockSpec((D_MODEL, D_MODEL), lambda j: (0, j)),
                  pl.BlockSpec((1, D_MODEL), lambda j: (0, j))],
        out_specs=pl.BlockSpec((n, D_MODEL), lambda j: (0, j)),
        out_shape=jax.ShapeDtypeStruct((n, 6 * D_MODEL), F32),
        compiler_params=_params("arbitrary"),
        name="adaln_mod",
    )(c_all, w_ada, b_ada)


def _group_mean_matrix():
    r = lax.broadcasted_iota(jnp.int32, (MXU_DIM, MXU_DIM), 0) // HEAD_DIM
    c = lax.broadcasted_iota(jnp.int32, (MXU_DIM, MXU_DIM), 1) // HEAD_DIM
    return jnp.where(r == c, 1.0 / HEAD_DIM, 0.0).astype(BF16)


def _group_rms(r, gmat):
    r2 = (r * r).astype(BF16)
    ms = jnp.concatenate(
        [_dot(r2[:, c * MXU_DIM:(c + 1) * MXU_DIM], gmat) for c in range(D_MODEL // MXU_DIM)], axis=1)
    return r * lax.rsqrt(ms + EPS)


def _inproj_kernel(x_ref, mod_ref, prev_ref, n1g_ref, w_ref, gq_ref, gk_ref, cw_ref,
                   qb_ref, kf_ref, kb_ref, vf_ref, vb_ref, yb_ref, sa_ref, sb_ref, tail_ref,
                   carry_ref, *, v_transposed):
    t = pl.program_id(1)
    tm = x_ref.shape[1]

    x = x_ref[0]
    h = (_rms_rows(x) * n1g_ref[...]) * (1.0 + mod_ref[0, 1:2, :]) + mod_ref[0, 0:1, :]
    h = h.astype(BF16)

    def proj(j):
        return _dot(h, w_ref[:, j * D_MODEL:(j + 1) * D_MODEL])

    gmat = _group_mean_matrix()

    rq = proj(0)
    rk = proj(1)
    qb_ref[0] = (_group_rms(rq, gmat) * gq_ref[...]).astype(BF16)

    v = proj(2)
    kn = _group_rms(rk, gmat) * gk_ref[...]
    kf_ref[0] = kn
    kb_ref[0] = kn.astype(BF16)
    vf_ref[0] = v
    if not v_transposed:
        vb_ref[0] = v.astype(BF16)

    @pl.when(t == 0)
    def _():
        carry_ref[0:2, :] = prev_ref[0]

    u = proj(5) * proj(3)
    p0 = carry_ref[0:1, :]
    p1 = carry_ref[1:2, :]
    row = lax.broadcasted_iota(jnp.int32, (tm, D_MODEL), 0)
    u1 = jnp.where(row == 0, p1, pltpu.roll(u, 1, 0))
    u2 = jnp.where(row == 0, p0, jnp.where(row == 1, p1, pltpu.roll(u, 2, 0)))
    cv = u2 * cw_ref[0:1, :] + u1 * cw_ref[1:2, :] + u * cw_ref[2:3, :]
    tail = u[tm - 2:tm, :]
    carry_ref[0:2, :] = tail
    tail_ref[0] = tail
    yb_ref[0] = (proj(4) * cv).astype(BF16)

    sa_ref[0] = jax.nn.sigmoid(proj(6)).astype(BF16)
    sb_ref[0] = jax.nn.sigmoid(proj(7)).astype(BF16)
    if v_transposed:
        vb_ref[0] = vf_ref[0].T.astype(BF16)


def _in_projection(x, mod, conv_prev, n1g, w_in, gq, gk, conv_w, tm, v_transposed):
    B, T, _ = x.shape
    tile = pl.BlockSpec((1, tm, D_MODEL), lambda b, t: (b, t, 0))
    per_b2 = pl.BlockSpec((1, CONV_WIDTH - 1, D_MODEL), lambda b, t: (b, 0, 0))
    row = _resident((1, D_MODEL))
    act_bf = jax.ShapeDtypeStruct((B, T, D_MODEL), BF16)
    act_f32 = jax.ShapeDtypeStruct((B, T, D_MODEL), F32)
    if v_transposed:
        vb_spec = pl.BlockSpec((1, D_MODEL, tm), lambda b, t: (b, 0, t))
        vb_shape = jax.ShapeDtypeStruct((B, D_MODEL, T), BF16)
    else:
        vb_spec, vb_shape = tile, act_bf
    return pl.pallas_call(
        functools.partial(_inproj_kernel, v_transposed=v_transposed),
        grid=(B, T // tm),
        in_specs=[tile,
                  pl.BlockSpec((1, 6, D_MODEL), lambda b, t: (b, 0, 0)),
                  per_b2,
                  row,
                  _resident((D_MODEL, N_IN_GROUPS * D_MODEL)),
                  row, row,
                  _resident((CONV_WIDTH, D_MODEL))],
        out_specs=[tile] * 4 + [vb_spec] + [tile] * 3 + [per_b2],
        out_shape=[act_bf, act_f32, act_bf, act_f32, vb_shape, act_bf, act_bf, act_bf,
                   jax.ShapeDtypeStruct((B, CONV_WIDTH - 1, D_MODEL), F32)],
        scratch_shapes=[pltpu.VMEM((8, D_MODEL), F32)],
        compiler_params=_params("arbitrary", "arbitrary"),
        name="in_projection",
    )(x, mod, conv_prev, n1g, w_in, gq, gk, conv_w)


def _stack_maps(q):
    lane = lax.broadcasted_iota(jnp.int32, q.shape, 1)
    zero = jnp.zeros_like(q)
    return jnp.concatenate([jnp.where(lane < HEAD_DIM, q, zero), jnp.where(lane >= HEAD_DIM, q, zero)], axis=0)


def _lambda_full(lam_ref, lam0):
    lv = lam_ref[...]
    a = jnp.sum(lv[0:1, :] * lv[1:2, :], axis=-1, keepdims=True)
    b = jnp.sum(lv[2:3, :] * lv[3:4, :], axis=-1, keepdims=True)
    return jnp.exp(a) - jnp.exp(b) + lam0


def _finish_head(o1, o2, lam, subg, lam0):
    o = o1 - lam * o2
    return (_rms_rows(o) * subg) * (1.0 - lam0)


def _attn_prompt_kernel(qa_ref, qb_ref, k_ref, vt_ref, lam_ref, subg_ref, olo_ref, ohi_ref,
                        qs_scr, bias_scr, m_scr, l_scr, acc_scr, s0_scr, s1_scr, *, lam0, running_max):
    i = pl.program_id(2)
    tq = qa_ref.shape[1]
    tk = s0_scr.shape[0]
    ks = tq // tk
    cols = 2 * tq
    n_heads = qa_ref.shape[2] // V_DIM
    n_tiles = k_ref.shape[1] // tq
    n_slots = (n_tiles + 1) * ks
    first_b = (i + 1) * ks
    surely_b = (n_tiles // 2) * ks

    def lanes(hh):
        return slice(hh * V_DIM, (hh + 1) * V_DIM)

    @pl.when((pl.program_id(0) == 0) & (pl.program_id(1) == 0) & (i == 0))
    def _():
        kchunk = lax.broadcasted_iota(jnp.int32, (tq, cols), 0) // CHUNK
        qchunk = (lax.broadcasted_iota(jnp.int32, (tq, cols), 1) % tq) // CHUNK
        bias_scr[...] = jnp.where(kchunk <= qchunk, 0.0, NEG_BIG)

    for hh in range(n_heads):
        qs_scr[2 * hh] = _stack_maps(qa_ref[0, :, lanes(hh)])
        qs_scr[2 * hh + 1] = _stack_maps(qb_ref[0, :, lanes(hh)])
    if running_max:
        m_scr[...] = jnp.full(m_scr.shape, NEG_BIG, F32)
    l_scr[...] = jnp.zeros(l_scr.shape, F32)
    acc_scr[...] = jnp.zeros(acc_scr.shape, F32)
    s_bufs = (s0_scr, s1_scr)

    def slot(hh, t):
        if t < ks:
            return 2 * hh, i * ks + t, t
        if t >= surely_b:
            return 2 * hh + 1, t - first_b, (t - (n_slots - ks) if t >= n_slots - ks else None)
        is_a = t < first_b
        return 2 * hh + jnp.where(is_a, 0, 1), jnp.where(is_a, t - ks, t - first_b), None

    def windows(d):
        if d is None or d == 0:
            return [(0, cols)]
        return [(d * tk, tq), (tq + d * tk, cols)]

    def gather(ref, sel, wins, axis):
        if len(wins) == 1:
            return ref[sel]
        parts = [ref[sel, lo:hi, :] if axis == 0 else ref[sel, :, lo:hi] for lo, hi in wins]
        return jnp.concatenate(parts, axis=axis)

    def scatter(ref, sel, wins, val):
        if len(wins) == 1:
            ref[sel] = val
            return
        off = 0
        for lo, hi in wins:
            ref[sel, :, lo:hi] = val[:, off:off + hi - lo]
            off += hi - lo

    def scores(hh, t, s_ref):
        sel, kt, d = slot(hh, t)
        wins = windows(d)
        width = sum(hi - lo for lo, hi in wins)
        k = k_ref[0, pl.ds(pl.multiple_of(kt * tk, tk), tk), lanes(hh)]
        s_ref[:, :width] = _dot_nt(k, gather(qs_scr, sel, wins, 0))

    def update(hh, t, s_ref):
        sel, kt, d = slot(hh, t)
        wins = windows(d)
        width = sum(hi - lo for lo, hi in wins)
        vt = vt_ref[0, lanes(hh), pl.ds(pl.multiple_of(kt * tk, tk), tk)]
        s = s_ref[:, :width]
        if d is not None:
            s = s + jnp.concatenate([bias_scr[d * tk:(d + 1) * tk, lo:hi] for lo, hi in wins], axis=1)
        if running_max:
            m_prev = gather(m_scr, sel, wins, 1)
            m_next = jnp.maximum(m_prev, jnp.max(s, axis=0, keepdims=True))
            alpha = jnp.exp2(m_prev - m_next)
            p = jnp.exp2(s - m_next)
            scatter(l_scr, sel, wins, alpha * gather(l_scr, sel, wins, 1) + jnp.sum(p, axis=0, keepdims=True))
            scatter(acc_scr, sel, wins, alpha * gather(acc_scr, sel, wins, 1) + _dot(vt, p.astype(BF16)))
            scatter(m_scr, sel, wins, m_next)
        else:
            p = jnp.exp2(s)
            scatter(l_scr, sel, wins, gather(l_scr, sel, wins, 1) + jnp.sum(p, axis=0, keepdims=True))
            scatter(acc_scr, sel, wins, gather(acc_scr, sel, wins, 1) + _dot(vt, p.astype(BF16)))

    lam = _lambda_full(lam_ref, lam0)

    def finish(hh, sel, o_ref):
        ot = acc_scr[2 * hh + sel] * (1.0 / l_scr[2 * hh + sel])
        o = ot[:, :tq] - lam * ot[:, tq:]
        o = (o * lax.rsqrt(jnp.mean(o * o, axis=0, keepdims=True) + EPS) * subg_ref[...]) * (1.0 - lam0)
        o_ref[0, :, lanes(hh)] = o.T.astype(o_ref.dtype)

    sweep = [(hh, t) for hh in range(n_heads) for t in range(n_slots)]
    scores(*sweep[0], s_bufs[0])
    for g, (hh, t) in enumerate(sweep):
        if g + 1 < len(sweep):
            scores(*sweep[g + 1], s_bufs[(g + 1) % 2])
        update(hh, t, s_bufs[g % 2])
        if t == surely_b - 1:
            finish(hh, 0, olo_ref)
        if t == n_slots - 1:
            finish(hh, 1, ohi_ref)


def _attention_prompt(qb, kb, vt, lam_vecs, subg_col, lam0, tq, tk, running_max):
    assert tq % tk == 0 and tk % CHUNK == 0
    B, S, _ = qb.shape
    n_tiles = S // tq
    half = n_tiles // 2
    hp = ATTN_HEADS_PER_STEP
    tile = (1, tq, hp * V_DIM)
    return pl.pallas_call(
        functools.partial(_attn_prompt_kernel, lam0=lam0, running_max=running_max),
        grid=(B, N_HEADS // hp, half),
        in_specs=[pl.BlockSpec(tile, lambda b, h, i: (b, i, h)),
                  pl.BlockSpec(tile, lambda b, h, i: (b, n_tiles - 1 - i, h)),
                  pl.BlockSpec((1, S, hp * V_DIM), lambda b, h, i: (b, 0, h)),
                  pl.BlockSpec((1, hp * V_DIM, S), lambda b, h, i: (b, h, 0)),
                  pl.BlockSpec((4, HEAD_DIM), lambda b, h, i: (0, 0)),
                  pl.BlockSpec((V_DIM, 1), lambda b, h, i: (0, 0))],
        out_specs=[pl.BlockSpec(tile, lambda b, h, i: (b, i, h)),
                   pl.BlockSpec(tile, lambda b, h, i: (b, half - 1 - i, h))],
        out_shape=[jax.ShapeDtypeStruct((B, S // 2, D_MODEL), BF16)] * 2,
        scratch_shapes=[pltpu.VMEM((2 * hp, 2 * tq, V_DIM), BF16),
                        pltpu.VMEM((tq, 2 * tq), F32),
                        pltpu.VMEM((2 * hp, 1, 2 * tq), F32),
                        pltpu.VMEM((2 * hp, 1, 2 * tq), F32),
                        pltpu.VMEM((2 * hp, V_DIM, 2 * tq), F32),
                        pltpu.VMEM((tk, 2 * tq), F32),
                        pltpu.VMEM((tk, 2 * tq), F32)],
        compiler_params=_params("arbitrary", "arbitrary", "arbitrary"),
        name="attn_prompt_running_max" if running_max else "attn_prompt_direct",
    )(qb, qb, kb, vt, lam_vecs, subg_col)


def _attn_sample_kernel(q_ref, kn_ref, vn_ref, kc_ref, vc_ref, lam_ref, subg_ref, o_ref, *, lam0):
    tq = q_ref.shape[1]
    lam = _lambda_full(lam_ref, lam0)
    for h in range(N_HEADS):
        cols = slice(h * V_DIM, (h + 1) * V_DIM)
        qs = _stack_maps(q_ref[0, :, cols])
        s_c = _dot_nt(qs, kc_ref[0, :, cols].astype(BF16))
        s_n = _dot_nt(qs, kn_ref[0, :, cols])
        m = jnp.maximum(jnp.max(s_c, axis=1, keepdims=True), jnp.max(s_n, axis=1, keepdims=True))
        p_c = jnp.exp2(s_c - m)
        p_n = jnp.exp2(s_n - m)
        l = jnp.sum(p_c, axis=1, keepdims=True) + jnp.sum(p_n, axis=1, keepdims=True)
        acc = (_dot(p_c.astype(BF16), vc_ref[0, :, cols].astype(BF16))
               + _dot(p_n.astype(BF16), vn_ref[0, :, cols]))
        o = acc / l
        o_ref[0, :, cols] = _finish_head(o[:tq], o[tq:], lam, subg_ref[...], lam0).astype(o_ref.dtype)


def _attention_sample(qb, kb, vb, cache_k, cache_v, lam_vecs, subg, lam0):
    B, T, _ = qb.shape
    P = cache_k.shape[1]
    new_spec = pl.BlockSpec((1, T, D_MODEL), lambda b: (b, 0, 0))
    past_spec = pl.BlockSpec((1, P, D_MODEL), lambda b: (b, 0, 0))
    return pl.pallas_call(
        functools.partial(_attn_sample_kernel, lam0=lam0),
        grid=(B,),
        in_specs=[new_spec, new_spec, new_spec, past_spec, past_spec,
                  pl.BlockSpec((4, HEAD_DIM), lambda b: (0, 0)),
                  pl.BlockSpec((1, V_DIM), lambda b: (0, 0))],
        out_specs=new_spec,
        out_shape=jax.ShapeDtypeStruct((B, T, D_MODEL), BF16),
        compiler_params=_params("arbitrary"),
        name="attn_sample",
    )(qb, kb, vb, cache_k, cache_v, lam_vecs, subg)


def _merge_kernel(x_ref, mod_ref, olo_ref, ohi_ref, yb_ref, sa_ref, sb_ref, wa_ref, wc_ref, wo_ref, x1_ref,
                  *, n_lo):
    o = jnp.where(pl.program_id(1) < n_lo, olo_ref[0], ohi_ref[0])
    y_a = _dot(o, wa_ref[...])
    y_b = _dot(yb_ref[0], wc_ref[...])
    m = sa_ref[0].astype(F32) * y_a + sb_ref[0].astype(F32) * y_b
    x1_ref[0] = x_ref[0] + mod_ref[0, 2:3, :] * _dot(m.astype(BF16), wo_ref[...])


def _merge(x, mod, o_lo, o_hi, yb, sa, sb, wa, wc, wo, tm):
    B, T, _ = x.shape
    n_lo = o_lo.shape[1] // tm
    tile = pl.BlockSpec((1, tm, D_MODEL), lambda b, t: (b, t, 0))
    wspec = _resident((D_MODEL, D_MODEL))
    return pl.pallas_call(
        functools.partial(_merge_kernel, n_lo=n_lo),
        grid=(B, T // tm),
        in_specs=[tile, pl.BlockSpec((1, 6, D_MODEL), lambda b, t: (b, 0, 0)),
                  pl.BlockSpec((1, tm, D_MODEL), lambda b, t: (b, jnp.minimum(t, n_lo - 1), 0)),
                  pl.BlockSpec((1, tm, D_MODEL), lambda b, t: (b, jnp.maximum(t - n_lo, 0), 0)),
                  tile, tile, tile, wspec, wspec, wspec],
        out_specs=tile,
        out_shape=jax.ShapeDtypeStruct((B, T, D_MODEL), F32),
        compiler_params=_params("arbitrary", "arbitrary"),
        name="branch_merge",
    )(x, mod, o_lo, o_hi, yb, sa, sb, wa, wc, wo)


def _ffn_chunks():
    step = 2 * MXU_DIM
    return [(c0, min(step, D_FF - c0)) for c0 in range(0, D_FF, step)]


def _ffn_kernel(x_ref, mod_ref, n2g_ref, wgu_ref, wd_ref, y_ref):
    x1 = x_ref[0]
    h2 = (_rms_rows(x1) * n2g_ref[...]) * (1.0 + mod_ref[0, 4:5, :]) + mod_ref[0, 3:4, :]
    h2 = h2.astype(BF16)
    acc = jnp.zeros(x1.shape, F32)
    for c0, cs in _ffn_chunks():
        g = _dot(h2, wgu_ref[:, c0:c0 + cs])
        up = _dot(h2, wgu_ref[:, D_FF + c0:D_FF + c0 + cs])
        a = ((g * jax.nn.sigmoid(g)) * up).astype(BF16)
        acc = acc + _dot(a, wd_ref[c0:c0 + cs, :])
    y_ref[0] = x1 + mod_ref[0, 5:6, :] * acc


def _ffn(x1, mod, n2g, wgu, wd, tm):
    B, T, _ = x1.shape
    tile = pl.BlockSpec((1, tm, D_MODEL), lambda b, t: (b, t, 0))
    return pl.pallas_call(
        _ffn_kernel,
        grid=(B, T // tm),
        in_specs=[tile, pl.BlockSpec((1, 6, D_MODEL), lambda b, t: (b, 0, 0)),
                  _resident((1, D_MODEL)),
                  _resident((D_MODEL, 2 * D_FF)),
                  _resident((D_FF, D_MODEL))],
        out_specs=tile,
        out_shape=jax.ShapeDtypeStruct((B, T, D_MODEL), F32),
        compiler_params=_params("arbitrary", "arbitrary"),
        name="swiglu_ffn",
    )(x1, mod, n2g, wgu, wd)


def _tile_rows(T):
    return min(T, 512)


def _layer(x, mod, past_k, past_v, conv_prev, lam0, w, attn_tiles):
    B, T, _ = x.shape
    tm = _tile_rows(T)
    qb, kf, kb, vf, vb, yb, sa, sb, tail = _in_projection(
        x, mod, conv_prev, w["n1g"], w["w_in"], w["gq"], w["gk"], w["conv_w"], tm, past_k is None)
    if past_k is None:
        score_bound = HEAD_DIM * jnp.max(jnp.abs(w["gq"])) * jnp.max(jnp.abs(w["gk"]))
        attend = functools.partial(_attention_prompt, qb, kb, vb, w["lam_vecs"], w["subg"].reshape(V_DIM, 1),
                                   lam0, *attn_tiles)
        o_lo, o_hi = lax.cond(score_bound <= SCORE_BOUND_MAX, lambda: attend(False), lambda: attend(True))
    else:
        o_lo = o_hi = _attention_sample(qb, kb, vb, past_k, past_v, w["lam_vecs"], w["subg"], lam0)
    x1 = _merge(x, mod, o_lo, o_hi, yb, sa, sb, w["wa"], w["wc"], w["wo"], tm)
    y = _ffn(x1, mod, w["n2g"], w["wgu"], w["wd"], tm)
    return (y, kf.reshape(B, T, N_HEADS, 2, HEAD_DIM), vf.reshape(B, T, N_HEADS, V_DIM), tail)


def kernel(x_prompt, x_sample, cache_k, cache_v, state_conv, c_prompt, c_sample, w_ada, b_ada, norm1_g, norm2_g, w_in, q_norm_g, k_norm_g, lambda_q1, lambda_k1, lambda_q2, lambda_k2, sub_norm_g, w_attn_out, conv_w, w_conv_out, w_out, w_gate_up, w_down):
    depth = w_ada.shape[0]
    B = x_prompt.shape[0]
    Bs, _, _ = x_sample.shape
    past = cache_k.shape[2]
    xp, xs = x_prompt, x_sample
    outs = [[] for _ in range(6)]
    c_all = jnp.concatenate([c_prompt, c_sample], axis=0)
    for l in range(depth):
        lam0 = _lambda_init(l)
        mod = _modulation(c_all, w_ada[l].astype(BF16), b_ada[l][None, :]).reshape(B + Bs, 6, D_MODEL)
        w = {
            "n1g": norm1_g[l][None, :], "n2g": norm2_g[l][None, :],
            "w_in": w_in[l].astype(BF16),
            "gq": jnp.tile(q_norm_g[l], D_MODEL // HEAD_DIM)[None, :] * (HEAD_DIM ** -0.5 * LOG2E),
            "gk": jnp.tile(k_norm_g[l], D_MODEL // HEAD_DIM)[None, :],
            "conv_w": conv_w[l],
            "lam_vecs": jnp.stack([lambda_q1[l], lambda_k1[l], lambda_q2[l], lambda_k2[l]]),
            "subg": sub_norm_g[l][None, :],
            "wa": w_attn_out[l].astype(BF16), "wc": w_conv_out[l].astype(BF16), "wo": w_out[l].astype(BF16),
            "wgu": w_gate_up[l].astype(BF16), "wd": w_down[l].astype(BF16),
        }
        zeros_prev = jnp.zeros((B, CONV_WIDTH - 1, D_MODEL), xp.dtype)
        xp, kp, vp, cp = _layer(xp, mod[:B], None, None, zeros_prev, lam0, w, (512, 256))
        xs, ksn, vsn, csn = _layer(xs, mod[B:], cache_k[l].reshape(Bs, past, D_MODEL),
                                   cache_v[l].reshape(Bs, past, D_MODEL), state_conv[l], lam0, w, None)
        for lst, val in zip(outs, (kp, vp, cp, ksn, vsn, csn)):
            lst.append(val)
    return (xp, xs) + tuple(jnp.stack(o) for o in outs)
```

```python
import functools
import math

import jax
import jax.numpy as jnp
from jax import lax
from jax.experimental import pallas as pl
from jax.experimental.pallas import tpu as pltpu

D_MODEL = 1024
N_HEADS = 8
HEAD_DIM = 64
V_DIM = 2 * HEAD_DIM
CONV_WIDTH = 3
CHUNK = 64
D_FF = 2816
EPS = 1e-6
N_IN_GROUPS = 8
LOG2E = math.log2(math.e)
NEG_BIG = -1e30
SCORE_BOUND_MAX = 32.0
ATTN_HEADS_PER_STEP = 2
INPROJ_SUB_ROWS = 256

LANES = 128
MXU_DIM = 256
VMEM_LIMIT = 56 * 1024 * 1024

F32 = jnp.float32
BF16 = jnp.bfloat16


def _lambda_init(layer_idx):
    return 0.8 - 0.6 * math.exp(-0.3 * layer_idx)


def _params(*sem):
    return pltpu.CompilerParams(dimension_semantics=sem, vmem_limit_bytes=VMEM_LIMIT)


def _resident(shape):
    return pl.BlockSpec(shape, lambda *_: (0,) * len(shape), pipeline_mode=pl.Buffered(1))


def _dot(a, b):
    return jnp.dot(a, b, preferred_element_type=F32)


def _dot_nt(a, b):
    return lax.dot_general(a, b, (((1,), (1,)), ((), ())), preferred_element_type=F32)


def _rms_rows(x):
    return x * lax.rsqrt(jnp.mean(x * x, axis=-1, keepdims=True) + EPS)


def _mod_kernel(c_ref, w_ref, b_ref, o_ref):
    c = c_ref[...]
    sc = (c * jax.nn.sigmoid(c)).astype(BF16)
    o_ref[...] = _dot(sc, w_ref[...]) + b_ref[...]


def _modulation(c_all, w_ada, b_ada):
    n = c_all.shape[0]
    return pl.pallas_call(
        _mod_kernel,
        grid=(6,),
        in_specs=[pl.BlockSpec((n, D_MODEL), lambda j: (0, 0)),
                  pl.BlockSpec((D_MODEL, D_MODEL), lambda j: (0, j)),
                  pl.BlockSpec((1, D_MODEL), lambda j: (0, j))],
        out_specs=pl.BlockSpec((n, D_MODEL), lambda j: (0, j)),
        out_shape=jax.ShapeDtypeStruct((n, 6 * D_MODEL), F32),
        compiler_params=_params("arbitrary"),
        name="adaln_mod",
    )(c_all, w_ada, b_ada)


def _group_mean_matrix():
    r = lax.broadcasted_iota(jnp.int32, (MXU_DIM, MXU_DIM), 0) // HEAD_DIM
    c = lax.broadcasted_iota(jnp.int32, (MXU_DIM, MXU_DIM), 1) // HEAD_DIM
    return jnp.where(r == c, 1.0 / HEAD_DIM, 0.0).astype(BF16)


def _group_rms(r, gmat):
    r2 = (r * r).astype(BF16)
    ms = jnp.concatenate(
        [_dot(r2[:, c * MXU_DIM:(c + 1) * MXU_DIM], gmat) for c in range(D_MODEL // MXU_DIM)], axis=1)
    return r * lax.rsqrt(ms + EPS)


def _inproj_kernel(x_ref, mod_ref, prev_ref, n1g_ref, w_ref, gq_ref, gk_ref, cw_ref,
                   qb_ref, kf_ref, kb_ref, vf_ref, vb_ref, yb_ref, sa_ref, sb_ref, tail_ref,
                   carry_ref, *, v_transposed):
    t = pl.program_id(1)
    tm = x_ref.shape[1]
    sub = min(tm, INPROJ_SUB_ROWS)
    gmat = _group_mean_matrix()

    @pl.when(t == 0)
    def _():
        carry_ref[0:2, :] = prev_ref[0]

    for r in range(tm // sub):
        rows = slice(r * sub, (r + 1) * sub)
        x = x_ref[0, rows, :]
        h = (_rms_rows(x) * n1g_ref[...]) * (1.0 + mod_ref[0, 1:2, :]) + mod_ref[0, 0:1, :]
        h = h.astype(BF16)

        def proj(j):
            return _dot(h, w_ref[:, j * D_MODEL:(j + 1) * D_MODEL])

        rq = proj(0)
        rk = proj(1)
        qb_ref[0, rows, :] = (_group_rms(rq, gmat) * gq_ref[...]).astype(BF16)

        v = proj(2)
        kn = _group_rms(rk, gmat) * gk_ref[...]
        kf_ref[0, rows, :] = kn
        kb_ref[0, rows, :] = kn.astype(BF16)
        vf_ref[0, rows, :] = v
        if not v_transposed:
            vb_ref[0, rows, :] = v.astype(BF16)

        u = proj(5) * proj(3)
        p0 = carry_ref[0:1, :]
        p1 = carry_ref[1:2, :]
        row = lax.broadcasted_iota(jnp.int32, (sub, D_MODEL), 0)
        u1 = jnp.where(row == 0, p1, pltpu.roll(u, 1, 0))
        u2 = jnp.where(row == 0, p0, jnp.where(row == 1, p1, pltpu.roll(u, 2, 0)))
        cv = u2 * cw_ref[0:1, :] + u1 * cw_ref[1:2, :] + u * cw_ref[2:3, :]
        tail = u[sub - 2:sub, :]
        carry_ref[0:2, :] = tail
        tail_ref[0] = tail
        yb_ref[0, rows, :] = (proj(4) * cv).astype(BF16)

        sa_ref[0, rows, :] = jax.nn.sigmoid(proj(6)).astype(BF16)
        sb_ref[0, rows, :] = jax.nn.sigmoid(proj(7)).astype(BF16)
    if v_transposed:
        vb_ref[0] = vf_ref[0].T.astype(BF16)


def _in_projection(x, mod, conv_prev, n1g, w_in, gq, gk, conv_w, tm, v_transposed):
    B, T, _ = x.shape
    tile = pl.BlockSpec((1, tm, D_MODEL), lambda b, t: (b, t, 0))
    per_b2 = pl.BlockSpec((1, CONV_WIDTH - 1, D_MODEL), lambda b, t: (b, 0, 0))
    row = _resident((1, D_MODEL))
    act_bf = jax.ShapeDtypeStruct((B, T, D_MODEL), BF16)
    act_f32 = jax.ShapeDtypeStruct((B, T, D_MODEL), F32)
    if v_transposed:
        vb_spec = pl.BlockSpec((1, D_MODEL, tm), lambda b, t: (b, 0, t))
        vb_shape = jax.ShapeDtypeStruct((B, D_MODEL, T), BF16)
    else:
        vb_spec, vb_shape = tile, act_bf
    return pl.pallas_call(
        functools.partial(_inproj_kernel, v_transposed=v_transposed),
        grid=(B, T // tm),
        in_specs=[tile,
                  pl.BlockSpec((1, 6, D_MODEL), lambda b, t: (b, 0, 0)),
                  per_b2,
                  row,
                  _resident((D_MODEL, N_IN_GROUPS * D_MODEL)),
                  row, row,
                  _resident((CONV_WIDTH, D_MODEL))],
        out_specs=[tile] * 4 + [vb_spec] + [tile] * 3 + [per_b2],
        out_shape=[act_bf, act_f32, act_bf, act_f32, vb_shape, act_bf, act_bf, act_bf,
                   jax.ShapeDtypeStruct((B, CONV_WIDTH - 1, D_MODEL), F32)],
        scratch_shapes=[pltpu.VMEM((8, D_MODEL), F32)],
        compiler_params=_params("arbitrary", "arbitrary"),
        name="in_projection",
    )(x, mod, conv_prev, n1g, w_in, gq, gk, conv_w)


def _stack_maps(q):
    lane = lax.broadcasted_iota(jnp.int32, q.shape, 1)
    zero = jnp.zeros_like(q)
    return jnp.concatenate([jnp.where(lane < HEAD_DIM, q, zero), jnp.where(lane >= HEAD_DIM, q, zero)], axis=0)


def _lambda_full(lam_ref, lam0):
    lv = lam_ref[...]
    a = jnp.sum(lv[0:1, :] * lv[1:2, :], axis=-1, keepdims=True)
    b = jnp.sum(lv[2:3, :] * lv[3:4, :], axis=-1, keepdims=True)
    return jnp.exp(a) - jnp.exp(b) + lam0


def _finish_head(o1, o2, lam, subg, lam0):
    o = o1 - lam * o2
    return (_rms_rows(o) * subg) * (1.0 - lam0)


def _attn_prompt_kernel(qa_ref, qb_ref, k_ref, vt_ref, lam_ref, subg_ref, olo_ref, ohi_ref,
                        qs_scr, bias_scr, m_scr, l_scr, acc_scr, s0_scr, s1_scr, *, lam0, running_max):
    i = pl.program_id(2)
    tq = qa_ref.shape[1]
    tk = s0_scr.shape[0]
    ks = tq // tk
    cols = 2 * tq
    n_heads = qa_ref.shape[2] // V_DIM
    n_tiles = k_ref.shape[1] // tq
    n_slots = (n_tiles + 1) * ks
    first_b = (i + 1) * ks
    surely_b = (n_tiles // 2) * ks

    def lanes(hh):
        return slice(hh * V_DIM, (hh + 1) * V_DIM)

    @pl.when((pl.program_id(0) == 0) & (pl.program_id(1) == 0) & (i == 0))
    def _():
        kchunk = lax.broadcasted_iota(jnp.int32, (tq, cols), 0) // CHUNK
        qchunk = (lax.broadcasted_iota(jnp.int32, (tq, cols), 1) % tq) // CHUNK
        bias_scr[...] = jnp.where(kchunk <= qchunk, 0.0, NEG_BIG)

    for hh in range(n_heads):
        qs_scr[2 * hh] = _stack_maps(qa_ref[0, :, lanes(hh)])
        qs_scr[2 * hh + 1] = _stack_maps(qb_ref[0, :, lanes(hh)])
    if running_max:
        m_scr[...] = jnp.full(m_scr.shape, NEG_BIG, F32)
    l_scr[...] = jnp.zeros(l_scr.shape, F32)
    acc_scr[...] = jnp.zeros(acc_scr.shape, F32)
    s_bufs = (s0_scr, s1_scr)

    def slot(hh, t):
        if t < ks:
            return 2 * hh, i * ks + t, t
        if t >= surely_b:
            return 2 * hh + 1, t - first_b, (t - (n_slots - ks) if t >= n_slots - ks else None)
        is_a = t < first_b
        return 2 * hh + jnp.where(is_a, 0, 1), jnp.where(is_a, t - ks, t - first_b), None

    def windows(d):
        if d is None or d == 0:
            return [(0, cols)]
        return [(d * tk, tq), (tq + d * tk, cols)]

    def gather(ref, sel, wins, axis):
        if len(wins) == 1:
            return ref[sel]
        parts = [ref[sel, lo:hi, :] if axis == 0 else ref[sel, :, lo:hi] for lo, hi in wins]
        return jnp.concatenate(parts, axis=axis)

    def scatter(ref, sel, wins, val):
        if len(wins) == 1:
            ref[sel] = val
            return
        off = 0
        for lo, hi in wins:
            ref[sel, :, lo:hi] = val[:, off:off + hi - lo]
            off += hi - lo

    def scores(hh, t, s_ref):
        sel, kt, d = slot(hh, t)
        wins = windows(d)
        width = sum(hi - lo for lo, hi in wins)
        k = k_ref[0, pl.ds(pl.multiple_of(kt * tk, tk), tk), lanes(hh)]
        s_ref[:, :width] = _dot_nt(k, gather(qs_scr, sel, wins, 0))

    def update(hh, t, s_ref):
        sel, kt, d = slot(hh, t)
        wins = windows(d)
        width = sum(hi - lo for lo, hi in wins)
        vt = vt_ref[0, lanes(hh), pl.ds(pl.multiple_of(kt * tk, tk), tk)]
        s = s_ref[:, :width]
        if d is not None:
            s = s + jnp.concatenate([bias_scr[d * tk:(d + 1) * tk, lo:hi] for lo, hi in wins], axis=1)
        if running_max:
            m_prev = gather(m_scr, sel, wins, 1)
            m_next = jnp.maximum(m_prev, jnp.max(s, axis=0, keepdims=True))
            alpha = jnp.exp2(m_prev - m_next)
            p = jnp.exp2(s - m_next)
            scatter(l_scr, sel, wins, alpha * gather(l_scr, sel, wins, 1) + jnp.sum(p, axis=0, keepdims=True))
            scatter(acc_scr, sel, wins, alpha * gather(acc_scr, sel, wins, 1) + _dot(vt, p.astype(BF16)))
            scatter(m_scr, sel, wins, m_next)
        else:
            p = jnp.exp2(s)
            scatter(l_scr, sel, wins, gather(l_scr, sel, wins, 1) + jnp.sum(p, axis=0, keepdims=True))
            scatter(acc_scr, sel, wins, gather(acc_scr, sel, wins, 1) + _dot(vt, p.astype(BF16)))

    lam = _lambda_full(lam_ref, lam0)

    def finish(hh, sel, o_ref):
        ot = acc_scr[2 * hh + sel] * (1.0 / l_scr[2 * hh + sel])
        o = ot[:, :tq] - lam * ot[:, tq:]
        o = (o * lax.rsqrt(jnp.mean(o * o, axis=0, keepdims=True) + EPS) * subg_ref[...]) * (1.0 - lam0)
        o_ref[0, :, lanes(hh)] = o.T.astype(o_ref.dtype)

    sweep = [(hh, t) for hh in range(n_heads) for t in range(n_slots)]
    scores(*sweep[0], s_bufs[0])
    for g, (hh, t) in enumerate(sweep):
        if g + 1 < len(sweep):
            scores(*sweep[g + 1], s_bufs[(g + 1) % 2])
        update(hh, t, s_bufs[g % 2])
        if t == surely_b - 1:
            finish(hh, 0, olo_ref)
        if t == n_slots - 1:
            finish(hh, 1, ohi_ref)


def _attention_prompt(qb, kb, vt, lam_vecs, subg_col, lam0, tq, tk, running_max):
    assert tq % tk == 0 and tk % CHUNK == 0
    B, S, _ = qb.shape
    n_tiles = S // tq
    half = n_tiles // 2
    hp = ATTN_HEADS_PER_STEP
    tile = (1, tq, hp * V_DIM)
    return pl.pallas_call(
        functools.partial(_attn_prompt_kernel, lam0=lam0, running_max=running_max),
        grid=(B, N_HEADS // hp, half),
        in_specs=[pl.BlockSpec(tile, lambda b, h, i: (b, i, h)),
                  pl.BlockSpec(tile, lambda b, h, i: (b, n_tiles - 1 - i, h)),
                  pl.BlockSpec((1, S, hp * V_DIM), lambda b, h, i: (b, 0, h)),
                  pl.BlockSpec((1, hp * V_DIM, S), lambda b, h, i: (b, h, 0)),
                  pl.BlockSpec((4, HEAD_DIM), lambda b, h, i: (0, 0)),
                  pl.BlockSpec((V_DIM, 1), lambda b, h, i: (0, 0))],
        out_specs=[pl.BlockSpec(tile, lambda b, h, i: (b, i, h)),
                   pl.BlockSpec(tile, lambda b, h, i: (b, half - 1 - i, h))],
        out_shape=[jax.ShapeDtypeStruct((B, S // 2, D_MODEL), BF16)] * 2,
        scratch_shapes=[pltpu.VMEM((2 * hp, 2 * tq, V_DIM), BF16),
                        pltpu.VMEM((tq, 2 * tq), F32),
                        pltpu.VMEM((2 * hp, 1, 2 * tq), F32),
                        pltpu.VMEM((2 * hp, 1, 2 * tq), F32),
                        pltpu.VMEM((2 * hp, V_DIM, 2 * tq), F32),
                        pltpu.VMEM((tk, 2 * tq), F32),
                        pltpu.VMEM((tk, 2 * tq), F32)],
        compiler_params=_params("arbitrary", "arbitrary", "arbitrary"),
        name="attn_prompt_running_max" if running_max else "attn_prompt_direct",
    )(qb, qb, kb, vt, lam_vecs, subg_col)


def _attn_sample_kernel(q_ref, kn_ref, vn_ref, kc_ref, vc_ref, lam_ref, subg_ref, o_ref, *, lam0):
    tq = q_ref.shape[1]
    lam = _lambda_full(lam_ref, lam0)
    for h in range(N_HEADS):
        cols = slice(h * V_DIM, (h + 1) * V_DIM)
        qs = _stack_maps(q_ref[0, :, cols])
        s_c = _dot_nt(qs, kc_ref[0, :, cols].astype(BF16))
        s_n = _dot_nt(qs, kn_ref[0, :, cols])
        m = jnp.maximum(jnp.max(s_c, axis=1, keepdims=True), jnp.max(s_n, axis=1, keepdims=True))
        p_c = jnp.exp2(s_c - m)
        p_n = jnp.exp2(s_n - m)
        l = jnp.sum(p_c, axis=1, keepdims=True) + jnp.sum(p_n, axis=1, keepdims=True)
        acc = (_dot(p_c.astype(BF16), vc_ref[0, :, cols].astype(BF16))
               + _dot(p_n.astype(BF16), vn_ref[0, :, cols]))
        o = acc / l
        o_ref[0, :, cols] = _finish_head(o[:tq], o[tq:], lam, subg_ref[...], lam0).astype(o_ref.dtype)


def _attention_sample(qb, kb, vb, cache_k, cache_v, lam_vecs, subg, lam0):
    B, T, _ = qb.shape
    P = cache_k.shape[1]
    new_spec = pl.BlockSpec((1, T, D_MODEL), lambda b: (b, 0, 0))
    past_spec = pl.BlockSpec((1, P, D_MODEL), lambda b: (b, 0, 0))
    return pl.pallas_call(
        functools.partial(_attn_sample_kernel, lam0=lam0),
        grid=(B,),
        in_specs=[new_spec, new_spec, new_spec, past_spec, past_spec,
                  pl.BlockSpec((4, HEAD_DIM), lambda b: (0, 0)),
                  pl.BlockSpec((1, V_DIM), lambda b: (0, 0))],
        out_specs=new_spec,
        out_shape=jax.ShapeDtypeStruct((B, T, D_MODEL), BF16),
        compiler_params=_params("arbitrary"),
        name="attn_sample",
    )(qb, kb, vb, cache_k, cache_v, lam_vecs, subg)


def _merge_kernel(x_ref, mod_ref, olo_ref, ohi_ref, yb_ref, sa_ref, sb_ref, wa_ref, wc_ref, wo_ref, x1_ref,
                  *, n_lo):
    o = jnp.where(pl.program_id(1) < n_lo, olo_ref[0], ohi_ref[0])
    y_a = _dot(o, wa_ref[...])
    y_b = _dot(yb_ref[0], wc_ref[...])
    m = sa_ref[0].astype(F32) * y_a + sb_ref[0].astype(F32) * y_b
    x1_ref[0] = x_ref[0] + mod_ref[0, 2:3, :] * _dot(m.astype(BF16), wo_ref[...])


def _merge(x, mod, o_lo, o_hi, yb, sa, sb, wa, wc, wo, tm):
    B, T, _ = x.shape
    n_lo = o_lo.shape[1] // tm
    tile = pl.BlockSpec((1, tm, D_MODEL), lambda b, t: (b, t, 0))
    wspec = _resident((D_MODEL, D_MODEL))
    return pl.pallas_call(
        functools.partial(_merge_kernel, n_lo=n_lo),
        grid=(B, T // tm),
        in_specs=[tile, pl.BlockSpec((1, 6, D_MODEL), lambda b, t: (b, 0, 0)),
                  pl.BlockSpec((1, tm, D_MODEL), lambda b, t: (b, jnp.minimum(t, n_lo - 1), 0)),
                  pl.BlockSpec((1, tm, D_MODEL), lambda b, t: (b, jnp.maximum(t - n_lo, 0), 0)),
                  tile, tile, tile, wspec, wspec, wspec],
        out_specs=tile,
        out_shape=jax.ShapeDtypeStruct((B, T, D_MODEL), F32),
        compiler_params=_params("arbitrary", "arbitrary"),
        name="branch_merge",
    )(x, mod, o_lo, o_hi, yb, sa, sb, wa, wc, wo)


def _ffn_chunks():
    step = 2 * MXU_DIM
    return [(c0, min(step, D_FF - c0)) for c0 in range(0, D_FF, step)]


def _ffn_kernel(x_ref, mod_ref, n2g_ref, wgu_ref, wd_ref, y_ref):
    x1 = x_ref[0]
    h2 = (_rms_rows(x1) * n2g_ref[...]) * (1.0 + mod_ref[0, 4:5, :]) + mod_ref[0, 3:4, :]
    h2 = h2.astype(BF16)
    acc = jnp.zeros(x1.shape, F32)
    for c0, cs in _ffn_chunks():
        g = _dot(h2, wgu_ref[:, c0:c0 + cs])
        up = _dot(h2, wgu_ref[:, D_FF + c0:D_FF + c0 + cs])
        a = ((g * jax.nn.sigmoid(g)) * up).astype(BF16)
        acc = acc + _dot(a, wd_ref[c0:c0 + cs, :])
    y_ref[0] = x1 + mod_ref[0, 5:6, :] * acc


def _ffn(x1, mod, n2g, wgu, wd, tm):
    B, T, _ = x1.shape
    tile = pl.BlockSpec((1, tm, D_MODEL), lambda b, t: (b, t, 0))
    return pl.pallas_call(
        _ffn_kernel,
        grid=(B, T // tm),
        in_specs=[tile, pl.BlockSpec((1, 6, D_MODEL), lambda b, t: (b, 0, 0)),
                  _resident((1, D_MODEL)),
                  _resident((D_MODEL, 2 * D_FF)),
                  _resident((D_FF, D_MODEL))],
        out_specs=tile,
        out_shape=jax.ShapeDtypeStruct((B, T, D_MODEL), F32),
        compiler_params=_params("arbitrary", "arbitrary"),
        name="swiglu_ffn",
    )(x1, mod, n2g, wgu, wd)


def _tile_rows(T):
    return min(T, 512)


def _layer(x, mod, past_k, past_v, conv_prev, lam0, w, attn_tiles):
    B, T, _ = x.shape
    tm = _tile_rows(T)
    qb, kf, kb, vf, vb, yb, sa, sb, tail = _in_projection(
        x, mod, conv_prev, w["n1g"], w["w_in"], w["gq"], w["gk"], w["conv_w"], tm, past_k is None)
    if past_k is None:
        score_bound = HEAD_DIM * jnp.max(jnp.abs(w["gq"])) * jnp.max(jnp.abs(w["gk"]))
        attend = functools.partial(_attention_prompt, qb, kb, vb, w["lam_vecs"], w["subg"].reshape(V_DIM, 1),
                                   lam0, *attn_tiles)
        o_lo, o_hi = lax.cond(score_bound <= SCORE_BOUND_MAX, lambda: attend(False), lambda: attend(True))
    else:
        o_lo = o_hi = _attention_sample(qb, kb, vb, past_k, past_v, w["lam_vecs"], w["subg"], lam0)
    x1 = _merge(x, mod, o_lo, o_hi, yb, sa, sb, w["wa"], w["wc"], w["wo"], tm)
    y = _ffn(x1, mod, w["n2g"], w["wgu"], w["wd"], tm)
    return (y, kf.reshape(B, T, N_HEADS, 2, HEAD_DIM), vf.reshape(B, T, N_HEADS, V_DIM), tail)


def kernel(x_prompt, x_sample, cache_k, cache_v, state_conv, c_prompt, c_sample, w_ada, b_ada, norm1_g, norm2_g, w_in, q_norm_g, k_norm_g, lambda_q1, lambda_k1, lambda_q2, lambda_k2, sub_norm_g, w_attn_out, conv_w, w_conv_out, w_out, w_gate_up, w_down):
    depth = w_ada.shape[0]
    B = x_prompt.shape[0]
    Bs, _, _ = x_sample.shape
    past = cache_k.shape[2]
    xp, xs = x_prompt, x_sample
    outs = [[] for _ in range(6)]
    c_all = jnp.concatenate([c_prompt, c_sample], axis=0)
    for l in range(depth):
        lam0 = _lambda_init(l)
        mod = _modulation(c_all, w_ada[l].astype(BF16), b_ada[l][None, :]).reshape(B + Bs, 6, D_MODEL)
        w = {
            "n1g": norm1_g[l][None, :], "n2g": norm2_g[l][None, :],
            "w_in": w_in[l].astype(BF16),
            "gq": jnp.tile(q_norm_g[l], D_MODEL // HEAD_DIM)[None, :] * (HEAD_DIM ** -0.5 * LOG2E),
            "gk": jnp.tile(k_norm_g[l], D_MODEL // HEAD_DIM)[None, :],
            "conv_w": conv_w[l],
            "lam_vecs": jnp.stack([lambda_q1[l], lambda_k1[l], lambda_q2[l], lambda_k2[l]]),
            "subg": sub_norm_g[l][None, :],
            "wa": w_attn_out[l].astype(BF16), "wc": w_conv_out[l].astype(BF16), "wo": w_out[l].astype(BF16),
            "wgu": w_gate_up[l].astype(BF16), "wd": w_down[l].astype(BF16),
        }
        zeros_prev = jnp.zeros((B, CONV_WIDTH - 1, D_MODEL), xp.dtype)
        xp, kp, vp, cp = _layer(xp, mod[:B], None, None, zeros_prev, lam0, w, (512, 256))
        xs, ksn, vsn, csn = _layer(xs, mod[B:], cache_k[l].reshape(Bs, past, D_MODEL),
                                   cache_v[l].reshape(Bs, past, D_MODEL), state_conv[l], lam0, w, None)
        for lst, val in zip(outs, (kp, vp, cp, ksn, vsn, csn)):
            lst.append(val)
    return (xp, xs) + tuple(jnp.stack(o) for o in outs)
```

```python
import functools
import math

import jax
import jax.numpy as jnp
from jax import lax
from jax.experimental import pallas as pl
from jax.experimental.pallas import tpu as pltpu

D_MODEL = 1024
N_HEADS = 8
HEAD_DIM = 64
V_DIM = 2 * HEAD_DIM
CONV_WIDTH = 3
CHUNK = 64
D_FF = 2816
EPS = 1e-6
N_IN_GROUPS = 8
LOG2E = math.log2(math.e)
NEG_BIG = -1e30
SCORE_BOUND_MAX = 32.0
ATTN_HEADS_PER_STEP = 2
INPROJ_SUB_ROWS = 256

LANES = 128
MXU_DIM = 256
VMEM_LIMIT = 56 * 1024 * 1024

F32 = jnp.float32
BF16 = jnp.bfloat16


def _lambda_init(layer_idx):
    return 0.8 - 0.6 * math.exp(-0.3 * layer_idx)


def _params(*sem):
    return pltpu.CompilerParams(dimension_semantics=sem, vmem_limit_bytes=VMEM_LIMIT)


def _resident(shape):
    return pl.BlockSpec(shape, lambda *_: (0,) * len(shape), pipeline_mode=pl.Buffered(1))


def _dot(a, b):
    return jnp.dot(a, b, preferred_element_type=F32)


def _dot_nt(a, b):
    return lax.dot_general(a, b, (((1,), (1,)), ((), ())), preferred_element_type=F32)


def _rms_rows(x):
    return x * lax.rsqrt(jnp.mean(x * x, axis=-1, keepdims=True) + EPS)


def _mod_kernel(c_ref, w_ref, b_ref, o_ref):
    c = c_ref[...]
    sc = (c * jax.nn.sigmoid(c)).astype(BF16)
    o_ref[...] = _dot(sc, w_ref[...].astype(BF16)) + b_ref[...]


def _modulation(c_all, w_ada, b_ada):
    n = c_all.shape[0]
    return pl.pallas_call(
        _mod_kernel,
        grid=(6,),
        in_specs=[pl.BlockSpec((n, D_MODEL), lambda j: (0, 0)),
                  pl.BlockSpec((D_MODEL, D_MODEL), lambda j: (0, j)),
                  pl.BlockSpec((1, D_MODEL), lambda j: (0, j))],
        out_specs=pl.BlockSpec((n, D_MODEL), lambda j: (0, j)),
        out_shape=jax.ShapeDtypeStruct((n, 6 * D_MODEL), F32),
        compiler_params=_params("arbitrary"),
        name="adaln_mod",
    )(c_all, w_ada, b_ada)


def _group_mean_matrix():
    r = lax.broadcasted_iota(jnp.int32, (MXU_DIM, MXU_DIM), 0) // HEAD_DIM
    c = lax.broadcasted_iota(jnp.int32, (MXU_DIM, MXU_DIM), 1) // HEAD_DIM
    return jnp.where(r == c, 1.0 / HEAD_DIM, 0.0).astype(BF16)


def _group_rms(r, gmat):
    r2 = (r * r).astype(BF16)
    ms = jnp.concatenate(
        [_dot(r2[:, c * MXU_DIM:(c + 1) * MXU_DIM], gmat) for c in range(D_MODEL // MXU_DIM)], axis=1)
    return r * lax.rsqrt(ms + EPS)


def _inproj_kernel(x_ref, mod_ref, prev_ref, n1g_ref, w_ref, gq_ref, gk_ref, cw_ref,
                   qb_ref, kf_ref, kb_ref, vf_ref, vb_ref, yb_ref, sa_ref, sb_ref, tail_ref,
                   carry_ref, *, v_transposed):
    t = pl.program_id(1)
    tm = x_ref.shape[1]
    sub = min(tm, INPROJ_SUB_ROWS)
    gmat = _group_mean_matrix()

    @pl.when(t == 0)
    def _():
        carry_ref[0:2, :] = prev_ref[0]

    for r in range(tm // sub):
        rows = slice(r * sub, (r + 1) * sub)
        x = x_ref[0, rows, :]
        h = (_rms_rows(x) * n1g_ref[...]) * (1.0 + mod_ref[0, 1:2, :]) + mod_ref[0, 0:1, :]
        h = h.astype(BF16)

        def proj(j):
            return _dot(h, w_ref[:, j * D_MODEL:(j + 1) * D_MODEL])

        rq = proj(0)
        rk = proj(1)
        qb_ref[0, rows, :] = (_group_rms(rq, gmat) * gq_ref[...]).astype(BF16)

        v = proj(2)
        kn = _group_rms(rk, gmat) * gk_ref[...]
        kf_ref[0, rows, :] = kn
        kb_ref[0, rows, :] = kn.astype(BF16)
        vf_ref[0, rows, :] = v
        if not v_transposed:
            vb_ref[0, rows, :] = v.astype(BF16)

        u = proj(5) * proj(3)
        p0 = carry_ref[0:1, :]
        p1 = carry_ref[1:2, :]
        row = lax.broadcasted_iota(jnp.int32, (sub, D_MODEL), 0)
        u1 = jnp.where(row == 0, p1, pltpu.roll(u, 1, 0))
        u2 = jnp.where(row == 0, p0, jnp.where(row == 1, p1, pltpu.roll(u, 2, 0)))
        cv = u2 * cw_ref[0:1, :] + u1 * cw_ref[1:2, :] + u * cw_ref[2:3, :]
        tail = u[sub - 2:sub, :]
        carry_ref[0:2, :] = tail
        tail_ref[0] = tail
        yb_ref[0, rows, :] = (proj(4) * cv).astype(BF16)

        sa_ref[0, rows, :] = jax.nn.sigmoid(proj(6)).astype(BF16)
        sb_ref[0, rows, :] = jax.nn.sigmoid(proj(7)).astype(BF16)
    if v_transposed:
        vb_ref[0] = vf_ref[0].T.astype(BF16)


def _in_projection(x, mod, conv_prev, n1g, w_in, gq, gk, conv_w, tm, v_transposed):
    B, T, _ = x.shape
    tile = pl.BlockSpec((1, tm, D_MODEL), lambda b, t: (b, t, 0))
    per_b2 = pl.BlockSpec((1, CONV_WIDTH - 1, D_MODEL), lambda b, t: (b, 0, 0))
    row = _resident((1, D_MODEL))
    act_bf = jax.ShapeDtypeStruct((B, T, D_MODEL), BF16)
    act_f32 = jax.ShapeDtypeStruct((B, T, D_MODEL), F32)
    if v_transposed:
        vb_spec = pl.BlockSpec((1, D_MODEL, tm), lambda b, t: (b, 0, t))
        vb_shape = jax.ShapeDtypeStruct((B, D_MODEL, T), BF16)
    else:
        vb_spec, vb_shape = tile, act_bf
    return pl.pallas_call(
        functools.partial(_inproj_kernel, v_transposed=v_transposed),
        grid=(B, T // tm),
        in_specs=[tile,
                  pl.BlockSpec((1, 6, D_MODEL), lambda b, t: (b, 0, 0)),
                  per_b2,
                  row,
                  _resident((D_MODEL, N_IN_GROUPS * D_MODEL)),
                  row, row,
                  _resident((CONV_WIDTH, D_MODEL))],
        out_specs=[tile] * 4 + [vb_spec] + [tile] * 3 + [per_b2],
        out_shape=[act_bf, act_f32, act_bf, act_f32, vb_shape, act_bf, act_bf, act_bf,
                   jax.ShapeDtypeStruct((B, CONV_WIDTH - 1, D_MODEL), F32)],
        scratch_shapes=[pltpu.VMEM((8, D_MODEL), F32)],
        compiler_params=_params("arbitrary", "arbitrary"),
        name="in_projection",
    )(x, mod, conv_prev, n1g, w_in, gq, gk, conv_w)


def _stack_maps(q):
    lane = lax.broadcasted_iota(jnp.int32, q.shape, 1)
    zero = jnp.zeros_like(q)
    return jnp.concatenate([jnp.where(lane < HEAD_DIM, q, zero), jnp.where(lane >= HEAD_DIM, q, zero)], axis=0)


def _lambda_full(lam_ref, lam0):
    lv = lam_ref[...]
    a = jnp.sum(lv[0:1, :] * lv[1:2, :], axis=-1, keepdims=True)
    b = jnp.sum(lv[2:3, :] * lv[3:4, :], axis=-1, keepdims=True)
    return jnp.exp(a) - jnp.exp(b) + lam0


def _finish_head(o1, o2, lam, subg, lam0):
    o = o1 - lam * o2
    return (_rms_rows(o) * subg) * (1.0 - lam0)


def _attn_prompt_kernel(qa_ref, qb_ref, k_ref, vt_ref, lam_ref, subg_ref, olo_ref, ohi_ref,
                        qs_scr, bias_scr, m_scr, l_scr, acc_scr, s0_scr, s1_scr, *, lam0, running_max):
    i = pl.program_id(2)
    tq = qa_ref.shape[1]
    tk = s0_scr.shape[0]
    ks = tq // tk
    cols = 2 * tq
    n_heads = qa_ref.shape[2] // V_DIM
    n_tiles = k_ref.shape[1] // tq
    n_slots = (n_tiles + 1) * ks
    first_b = (i + 1) * ks
    surely_b = (n_tiles // 2) * ks

    def lanes(hh):
        return slice(hh * V_DIM, (hh + 1) * V_DIM)

    @pl.when((pl.program_id(0) == 0) & (pl.program_id(1) == 0) & (i == 0))
    def _():
        kchunk = lax.broadcasted_iota(jnp.int32, (tq, cols), 0) // CHUNK
        qchunk = (lax.broadcasted_iota(jnp.int32, (tq, cols), 1) % tq) // CHUNK
        bias_scr[...] = jnp.where(kchunk <= qchunk, 0.0, NEG_BIG)

    for hh in range(n_heads):
        qs_scr[2 * hh] = _stack_maps(qa_ref[0, :, lanes(hh)])
        qs_scr[2 * hh + 1] = _stack_maps(qb_ref[0, :, lanes(hh)])
    if running_max:
        m_scr[...] = jnp.full(m_scr.shape, NEG_BIG, F32)
    l_scr[...] = jnp.zeros(l_scr.shape, F32)
    acc_scr[...] = jnp.zeros(acc_scr.shape, F32)
    s_bufs = (s0_scr, s1_scr)

    def slot(hh, t):
        if t < ks:
            return 2 * hh, i * ks + t, t
        if t >= surely_b:
            return 2 * hh + 1, t - first_b, (t - (n_slots - ks) if t >= n_slots - ks else None)
        is_a = t < first_b
        return 2 * hh + jnp.where(is_a, 0, 1), jnp.where(is_a, t - ks, t - first_b), None

    def windows(d):
        if d is None or d == 0:
            return [(0, cols)]
        return [(d * tk, tq), (tq + d * tk, cols)]

    def gather(ref, sel, wins, axis):
        if len(wins) == 1:
            return ref[sel]
        parts = [ref[sel, lo:hi, :] if axis == 0 else ref[sel, :, lo:hi] for lo, hi in wins]
        return jnp.concatenate(parts, axis=axis)

    def scatter(ref, sel, wins, val):
        if len(wins) == 1:
            ref[sel] = val
            return
        off = 0
        for lo, hi in wins:
            ref[sel, :, lo:hi] = val[:, off:off + hi - lo]
            off += hi - lo

    def scores(hh, t, s_ref):
        sel, kt, d = slot(hh, t)
        wins = windows(d)
        width = sum(hi - lo for lo, hi in wins)
        k = k_ref[0, pl.ds(pl.multiple_of(kt * tk, tk), tk), lanes(hh)]
        s_ref[:, :width] = _dot_nt(k, gather(qs_scr, sel, wins, 0))

    def update(hh, t, s_ref):
        sel, kt, d = slot(hh, t)
        wins = windows(d)
        width = sum(hi - lo for lo, hi in wins)
        vt = vt_ref[0, lanes(hh), pl.ds(pl.multiple_of(kt * tk, tk), tk)]
        s = s_ref[:, :width]
        if d is not None:
            s = s + jnp.concatenate([bias_scr[d * tk:(d + 1) * tk, lo:hi] for lo, hi in wins], axis=1)
        if running_max:
            m_prev = gather(m_scr, sel, wins, 1)
            m_next = jnp.maximum(m_prev, jnp.max(s, axis=0, keepdims=True))
            alpha = jnp.exp2(m_prev - m_next)
            p = jnp.exp2(s - m_next)
            scatter(l_scr, sel, wins, alpha * gather(l_scr, sel, wins, 1) + jnp.sum(p, axis=0, keepdims=True))
            scatter(acc_scr, sel, wins, alpha * gather(acc_scr, sel, wins, 1) + _dot(vt, p.astype(BF16)))
            scatter(m_scr, sel, wins, m_next)
        else:
            p = jnp.exp2(s)
            scatter(l_scr, sel, wins, gather(l_scr, sel, wins, 1) + jnp.sum(p, axis=0, keepdims=True))
            scatter(acc_scr, sel, wins, gather(acc_scr, sel, wins, 1) + _dot(vt, p.astype(BF16)))

    lam = _lambda_full(lam_ref, lam0)

    def finish(hh, sel, o_ref):
        ot = acc_scr[2 * hh + sel] * (1.0 / l_scr[2 * hh + sel])
        o = ot[:, :tq] - lam * ot[:, tq:]
        o = (o * lax.rsqrt(jnp.mean(o * o, axis=0, keepdims=True) + EPS) * subg_ref[...]) * (1.0 - lam0)
        o_ref[0, :, lanes(hh)] = o.T.astype(o_ref.dtype)

    sweep = [(hh, t) for hh in range(n_heads) for t in range(n_slots)]
    scores(*sweep[0], s_bufs[0])
    for g, (hh, t) in enumerate(sweep):
        if g + 1 < len(sweep):
            scores(*sweep[g + 1], s_bufs[(g + 1) % 2])
        update(hh, t, s_bufs[g % 2])
        if t == surely_b - 1:
            finish(hh, 0, olo_ref)
        if t == n_slots - 1:
            finish(hh, 1, ohi_ref)


def _attention_prompt(qb, kb, vt, lam_vecs, subg_col, lam0, tq, tk, running_max):
    assert tq % tk == 0 and tk % CHUNK == 0
    B, S, _ = qb.shape
    n_tiles = S // tq
    half = n_tiles // 2
    hp = ATTN_HEADS_PER_STEP
    tile = (1, tq, hp * V_DIM)
    return pl.pallas_call(
        functools.partial(_attn_prompt_kernel, lam0=lam0, running_max=running_max),
        grid=(B, N_HEADS // hp, half),
        in_specs=[pl.BlockSpec(tile, lambda b, h, i: (b, i, h)),
                  pl.BlockSpec(tile, lambda b, h, i: (b, n_tiles - 1 - i, h)),
                  pl.BlockSpec((1, S, hp * V_DIM), lambda b, h, i: (b, 0, h)),
                  pl.BlockSpec((1, hp * V_DIM, S), lambda b, h, i: (b, h, 0)),
                  pl.BlockSpec((4, HEAD_DIM), lambda b, h, i: (0, 0)),
                  pl.BlockSpec((V_DIM, 1), lambda b, h, i: (0, 0))],
        out_specs=[pl.BlockSpec(tile, lambda b, h, i: (b, i, h)),
                   pl.BlockSpec(tile, lambda b, h, i: (b, half - 1 - i, h))],
        out_shape=[jax.ShapeDtypeStruct((B, S // 2, D_MODEL), BF16)] * 2,
        scratch_shapes=[pltpu.VMEM((2 * hp, 2 * tq, V_DIM), BF16),
                        pltpu.VMEM((tq, 2 * tq), F32),
                        pltpu.VMEM((2 * hp, 1, 2 * tq), F32),
                        pltpu.VMEM((2 * hp, 1, 2 * tq), F32),
                        pltpu.VMEM((2 * hp, V_DIM, 2 * tq), F32),
                        pltpu.VMEM((tk, 2 * tq), F32),
                        pltpu.VMEM((tk, 2 * tq), F32)],
        compiler_params=_params("arbitrary", "arbitrary", "arbitrary"),
        name="attn_prompt_running_max" if running_max else "attn_prompt_direct",
    )(qb, qb, kb, vt, lam_vecs, subg_col)


def _attn_sample_kernel(q_ref, kn_ref, vn_ref, kc_ref, vc_ref, lam_ref, subg_ref, o_ref, *, lam0):
    tq = q_ref.shape[1]
    lam = _lambda_full(lam_ref, lam0)
    for h in range(N_HEADS):
        cols = slice(h * V_DIM, (h + 1) * V_DIM)
        qs = _stack_maps(q_ref[0, :, cols])
        s_c = _dot_nt(qs, kc_ref[0, :, cols].astype(BF16))
        s_n = _dot_nt(qs, kn_ref[0, :, cols])
        m = jnp.maximum(jnp.max(s_c, axis=1, keepdims=True), jnp.max(s_n, axis=1, keepdims=True))
        p_c = jnp.exp2(s_c - m)
        p_n = jnp.exp2(s_n - m)
        l = jnp.sum(p_c, axis=1, keepdims=True) + jnp.sum(p_n, axis=1, keepdims=True)
        acc = (_dot(p_c.astype(BF16), vc_ref[0, :, cols].astype(BF16))
               + _dot(p_n.astype(BF16), vn_ref[0, :, cols]))
        o = acc / l
        o_ref[0, :, cols] = _finish_head(o[:tq], o[tq:], lam, subg_ref[...], lam0).astype(o_ref.dtype)


def _attention_sample(qb, kb, vb, cache_k, cache_v, lam_vecs, subg, lam0):
    B, T, _ = qb.shape
    P = cache_k.shape[1]
    new_spec = pl.BlockSpec((1, T, D_MODEL), lambda b: (b, 0, 0))
    past_spec = pl.BlockSpec((1, P, D_MODEL), lambda b: (b, 0, 0))
    return pl.pallas_call(
        functools.partial(_attn_sample_kernel, lam0=lam0),
        grid=(B,),
        in_specs=[new_spec, new_spec, new_spec, past_spec, past_spec,
                  pl.BlockSpec((4, HEAD_DIM), lambda b: (0, 0)),
                  pl.BlockSpec((1, V_DIM), lambda b: (0, 0))],
        out_specs=new_spec,
        out_shape=jax.ShapeDtypeStruct((B, T, D_MODEL), BF16),
        compiler_params=_params("arbitrary"),
        name="attn_sample",
    )(qb, kb, vb, cache_k, cache_v, lam_vecs, subg)


def _merge_kernel(x_ref, mod_ref, olo_ref, ohi_ref, yb_ref, sa_ref, sb_ref, wa_ref, wc_ref, wo_ref, x1_ref,
                  *, n_lo):
    o = jnp.where(pl.program_id(1) < n_lo, olo_ref[0], ohi_ref[0])
    y_a = _dot(o, wa_ref[...].astype(BF16))
    y_b = _dot(yb_ref[0], wc_ref[...].astype(BF16))
    m = sa_ref[0].astype(F32) * y_a + sb_ref[0].astype(F32) * y_b
    x1_ref[0] = x_ref[0] + mod_ref[0, 2:3, :] * _dot(m.astype(BF16), wo_ref[...].astype(BF16))


def _merge(x, mod, o_lo, o_hi, yb, sa, sb, wa, wc, wo, tm):
    B, T, _ = x.shape
    n_lo = o_lo.shape[1] // tm
    tile = pl.BlockSpec((1, tm, D_MODEL), lambda b, t: (b, t, 0))
    wspec = _resident((D_MODEL, D_MODEL))
    return pl.pallas_call(
        functools.partial(_merge_kernel, n_lo=n_lo),
        grid=(B, T // tm),
        in_specs=[tile, pl.BlockSpec((1, 6, D_MODEL), lambda b, t: (b, 0, 0)),
                  pl.BlockSpec((1, tm, D_MODEL), lambda b, t: (b, jnp.minimum(t, n_lo - 1), 0)),
                  pl.BlockSpec((1, tm, D_MODEL), lambda b, t: (b, jnp.maximum(t - n_lo, 0), 0)),
                  tile, tile, tile, wspec, wspec, wspec],
        out_specs=tile,
        out_shape=jax.ShapeDtypeStruct((B, T, D_MODEL), F32),
        compiler_params=_params("arbitrary", "arbitrary"),
        name="branch_merge",
    )(x, mod, o_lo, o_hi, yb, sa, sb, wa, wc, wo)


def _ffn_chunks():
    step = 2 * MXU_DIM
    return [(c0, min(step, D_FF - c0)) for c0 in range(0, D_FF, step)]


def _ffn_kernel(x_ref, mod_ref, n2g_ref, wgu_ref, wd_ref, y_ref):
    x1 = x_ref[0]
    h2 = (_rms_rows(x1) * n2g_ref[...]) * (1.0 + mod_ref[0, 4:5, :]) + mod_ref[0, 3:4, :]
    h2 = h2.astype(BF16)
    acc = jnp.zeros(x1.shape, F32)
    for c0, cs in _ffn_chunks():
        g = _dot(h2, wgu_ref[:, c0:c0 + cs])
        up = _dot(h2, wgu_ref[:, D_FF + c0:D_FF + c0 + cs])
        a = ((g * jax.nn.sigmoid(g)) * up).astype(BF16)
        acc = acc + _dot(a, wd_ref[c0:c0 + cs, :])
    y_ref[0] = x1 + mod_ref[0, 5:6, :] * acc


def _ffn(x1, mod, n2g, wgu, wd, tm):
    B, T, _ = x1.shape
    tile = pl.BlockSpec((1, tm, D_MODEL), lambda b, t: (b, t, 0))
    return pl.pallas_call(
        _ffn_kernel,
        grid=(B, T // tm),
        in_specs=[tile, pl.BlockSpec((1, 6, D_MODEL), lambda b, t: (b, 0, 0)),
                  _resident((1, D_MODEL)),
                  _resident((D_MODEL, 2 * D_FF)),
                  _resident((D_FF, D_MODEL))],
        out_specs=tile,
        out_shape=jax.ShapeDtypeStruct((B, T, D_MODEL), F32),
        compiler_params=_params("arbitrary", "arbitrary"),
        name="swiglu_ffn",
    )(x1, mod, n2g, wgu, wd)


def _tile_rows(T):
    return min(T, 512)


def _layer(x, mod, past_k, past_v, conv_prev, lam0, w, attn_tiles):
    B, T, _ = x.shape
    tm = _tile_rows(T)
    qb, kf, kb, vf, vb, yb, sa, sb, tail = _in_projection(
        x, mod, conv_prev, w["n1g"], w["w_in"], w["gq"], w["gk"], w["conv_w"], tm, past_k is None)
    if past_k is None:
        score_bound = HEAD_DIM * jnp.max(jnp.abs(w["gq"])) * jnp.max(jnp.abs(w["gk"]))
        attend = functools.partial(_attention_prompt, qb, kb, vb, w["lam_vecs"], w["subg"].reshape(V_DIM, 1),
                                   lam0, *attn_tiles)
        o_lo, o_hi = lax.cond(score_bound <= SCORE_BOUND_MAX, lambda: attend(False), lambda: attend(True))
    else:
        o_lo = o_hi = _attention_sample(qb, kb, vb, past_k, past_v, w["lam_vecs"], w["subg"], lam0)
    x1 = _merge(x, mod, o_lo, o_hi, yb, sa, sb, w["wa"], w["wc"], w["wo"], tm)
    y = _ffn(x1, mod, w["n2g"], w["wgu"], w["wd"], tm)
    return (y, kf.reshape(B, T, N_HEADS, 2, HEAD_DIM), vf.reshape(B, T, N_HEADS, V_DIM), tail)


def kernel(x_prompt, x_sample, cache_k, cache_v, state_conv, c_prompt, c_sample, w_ada, b_ada, norm1_g, norm2_g, w_in, q_norm_g, k_norm_g, lambda_q1, lambda_k1, lambda_q2, lambda_k2, sub_norm_g, w_attn_out, conv_w, w_conv_out, w_out, w_gate_up, w_down):
    depth = w_ada.shape[0]
    B = x_prompt.shape[0]
    Bs, _, _ = x_sample.shape
    past = cache_k.shape[2]
    xp, xs = x_prompt, x_sample
    outs = [[] for _ in range(6)]
    c_all = jnp.concatenate([c_prompt, c_sample], axis=0)
    for l in range(depth):
        lam0 = _lambda_init(l)
        mod = _modulation(c_all, w_ada[l], b_ada[l][None, :]).reshape(B + Bs, 6, D_MODEL)
        w = {
            "n1g": norm1_g[l][None, :], "n2g": norm2_g[l][None, :],
            "w_in": w_in[l].astype(BF16),
            "gq": jnp.tile(q_norm_g[l], D_MODEL // HEAD_DIM)[None, :] * (HEAD_DIM ** -0.5 * LOG2E),
            "gk": jnp.tile(k_norm_g[l], D_MODEL // HEAD_DIM)[None, :],
            "conv_w": conv_w[l],
            "lam_vecs": jnp.stack([lambda_q1[l], lambda_k1[l], lambda_q2[l], lambda_k2[l]]),
            "subg": sub_norm_g[l][None, :],
            "wa": w_attn_out[l], "wc": w_conv_out[l], "wo": w_out[l],
            "wgu": w_gate_up[l].astype(BF16), "wd": w_down[l].astype(BF16),
        }
        zeros_prev = jnp.zeros((B, CONV_WIDTH - 1, D_MODEL), xp.dtype)
        xp, kp, vp, cp = _layer(xp, mod[:B], None, None, zeros_prev, lam0, w, (512, 256))
        xs, ksn, vsn, csn = _layer(xs, mod[B:], cache_k[l].reshape(Bs, past, D_MODEL),
                                   cache_v[l].reshape(Bs, past, D_MODEL), state_conv[l], lam0, w, None)
        for lst, val in zip(outs, (kp, vp, cp, ksn, vsn, csn)):
            lst.append(val)
    return (xp, xs) + tuple(jnp.stack(o) for o in outs)
```
